```python
import jax, jax.numpy as jnp
from jax import lax
import numpy as np

D_MODEL = 2048
BATCH = 16
SEQ = 2048
DEPTH = 1
DEC_BATCH = 2
DEC_SEQ = 4096
PAST_LEN = 128

D_MIX = D_MODEL
D_CONF = D_MIX // 2
D_SC = D_MIX - D_CONF
CONF_KERNEL = 31
SC_KERNEL = 3
D_IN_COLS = 2 * D_CONF + 3 * D_SC
N_EXPERTS = 16
EC_CAPACITY_FACTOR = 2
D_EXPERT = D_MODEL
EPS = 1e-6

kernel_name = "hybrid_conv_ec_moe_encoder"


def rmsnorm(x, g):
    xf = x.astype(jnp.float32)
    y = xf * lax.rsqrt(jnp.mean(xf * xf, axis=-1, keepdims=True) + EPS)
    return (y * g.astype(jnp.float32)).astype(x.dtype)


def layernorm(x, g, b):
    xf = x.astype(jnp.float32)
    mu = jnp.mean(xf, axis=-1, keepdims=True)
    var = jnp.mean(jnp.square(xf - mu), axis=-1, keepdims=True)
    y = (xf - mu) * lax.rsqrt(var + EPS)
    return (y * g.astype(jnp.float32) + b.astype(jnp.float32)).astype(x.dtype)


def depthwise_conv_centred(x, w, b):
    k = w.shape[0]
    pad = (k - 1) // 2
    y = lax.conv_general_dilated(
        x, w[:, None, :].astype(x.dtype), window_strides=(1,),
        padding=[(pad, k - 1 - pad)],
        dimension_numbers=('NWC', 'WIO', 'NWC'),
        feature_group_count=x.shape[-1])
    return y + b.astype(x.dtype)


def token_mixer(x, g_norm1, w_in, conf_dw_w, conf_dw_b, conf_ln_g, conf_ln_b,
                sc_dw_w, sc_dw_b, g_out_conf, g_out_sc, w_out):
    h = rmsnorm(x, g_norm1)
    p = h @ w_in.astype(x.dtype)
    a_val, a_gate, sc_bg, sc_cg, sc_v = jnp.split(
        p, [D_CONF, 2 * D_CONF, 2 * D_CONF + D_SC, 2 * D_CONF + 2 * D_SC], axis=-1)
    a = a_val * jax.nn.sigmoid(a_gate)
    a = depthwise_conv_centred(a, conf_dw_w, conf_dw_b)
    a = jax.nn.silu(layernorm(a, conf_ln_g, conf_ln_b))
    s = sc_bg * depthwise_conv_centred(sc_cg * sc_v, sc_dw_w, sc_dw_b)
    y = jnp.concatenate([rmsnorm(a, g_out_conf), rmsnorm(s, g_out_sc)], axis=-1)
    return x + y @ w_out.astype(x.dtype)


def expert_choice_ffn(x, g_norm2, w_router, w_gate, w_up, w_down):
    b, s, d = x.shape
    n = b * s
    cap = EC_CAPACITY_FACTOR * n // N_EXPERTS
    h = rmsnorm(x, g_norm2).reshape(n, d)
    logits = (h @ w_router.astype(x.dtype)).astype(jnp.float32)
    aff = jax.nn.softmax(logits, axis=-1)
    gates, idx = lax.top_k(aff.T, cap)
    xe = h[idx]
    hid = jax.nn.silu(jnp.einsum('ecd,edf->ecf', xe, w_gate.astype(x.dtype))) * \
        jnp.einsum('ecd,edf->ecf', xe, w_up.astype(x.dtype))
    ye = jnp.einsum('ecf,efd->ecd', hid, w_down.astype(x.dtype)) * gates[..., None].astype(x.dtype)
    out = jnp.zeros((n, d), x.dtype).at[idx.reshape(-1)].add(ye.reshape(-1, d))
    return x + out.reshape(b, s, d)


def trunk(x, g_norm1, w_in, conf_dw_w, conf_dw_b, conf_ln_g, conf_ln_b,
          sc_dw_w, sc_dw_b, g_out_conf, g_out_sc, w_out,
          g_norm2, w_router, w_gate, w_up, w_down, g_final):
    for l in range(DEPTH):
        x = token_mixer(x, g_norm1[l], w_in[l], conf_dw_w[l], conf_dw_b[l], conf_ln_g[l], conf_ln_b[l],
                        sc_dw_w[l], sc_dw_b[l], g_out_conf[l], g_out_sc[l], w_out[l])
        x = expert_choice_ffn(x, g_norm2[l], w_router[l], w_gate[l], w_up[l], w_down[l])
    return rmsnorm(x, g_final)


def setup_inputs(seed: int = 0) -> dict:
    key = jax.random.key(seed)
    ks = jax.random.split(key, 20)
    f32 = jnp.float32

    def nrm(k, shape, scale):
        return jax.random.normal(k, shape, f32) * scale

    def gain(k, shape):
        return 1.0 + 0.02 * jax.random.normal(k, shape, f32)

    return {
        "x_prompt": jax.random.normal(ks[0], (BATCH, SEQ, D_MODEL), f32),
        "x_sample": jax.random.normal(ks[1], (DEC_BATCH, DEC_SEQ, D_MODEL), f32),
        "g_norm1": gain(ks[2], (DEPTH, D_MODEL)),
        "w_in": nrm(ks[3], (DEPTH, D_MODEL, D_IN_COLS), D_MODEL ** -0.5),
        "conf_dw_w": nrm(ks[4], (DEPTH, CONF_KERNEL, D_CONF), CONF_KERNEL ** -0.5),
        "conf_dw_b": nrm(ks[5], (DEPTH, D_CONF), 0.02),
        "conf_ln_g": gain(ks[6], (DEPTH, D_CONF)),
        "conf_ln_b": nrm(ks[7], (DEPTH, D_CONF), 0.02),
        "sc_dw_w": nrm(ks[8], (DEPTH, SC_KERNEL, D_SC), SC_KERNEL ** -0.5),
        "sc_dw_b": nrm(ks[9], (DEPTH, D_SC), 0.02),
        "g_out_conf": gain(ks[10], (DEPTH, D_CONF)),
        "g_out_sc": gain(ks[11], (DEPTH, D_SC)),
        "w_out": nrm(ks[12], (DEPTH, D_MIX, D_MODEL), D_MIX ** -0.5),
        "g_norm2": gain(ks[13], (DEPTH, D_MODEL)),
        "w_router": nrm(ks[14], (DEPTH, D_MODEL, N_EXPERTS), D_MODEL ** -0.5),
        "w_gate": nrm(ks[15], (DEPTH, N_EXPERTS, D_MODEL, D_EXPERT), D_MODEL ** -0.5),
        "w_up": nrm(ks[16], (DEPTH, N_EXPERTS, D_MODEL, D_EXPERT), D_MODEL ** -0.5),
        "w_down": nrm(ks[17], (DEPTH, N_EXPERTS, D_EXPERT, D_MODEL), D_EXPERT ** -0.5),
        "g_final": gain(ks[18], (D_MODEL,)),
    }


def reference(x_prompt, x_sample, g_norm1, w_in, conf_dw_w, conf_dw_b, conf_ln_g, conf_ln_b,
              sc_dw_w, sc_dw_b, g_out_conf, g_out_sc, w_out,
              g_norm2, w_router, w_gate, w_up, w_down, g_final):
    y_prompt = trunk(x_prompt, g_norm1, w_in, conf_dw_w, conf_dw_b, conf_ln_g, conf_ln_b,
                     sc_dw_w, sc_dw_b, g_out_conf, g_out_sc, w_out,
                     g_norm2, w_router, w_gate, w_up, w_down, g_final)
    y_sample = trunk(x_sample, g_norm1, w_in, conf_dw_w, conf_dw_b, conf_ln_g, conf_ln_b,
                     sc_dw_w, sc_dw_b, g_out_conf, g_out_sc, w_out,
                     g_norm2, w_router, w_gate, w_up, w_down, g_final)
    return (y_prompt, y_sample)
```

```python
import functools

import jax
import jax.numpy as jnp
from jax import lax
from jax.experimental import pallas as pl
from jax.experimental.pallas import tpu as pltpu

F32 = jnp.float32
BF16 = jnp.bfloat16
I32 = jnp.int32

EPS = 1e-6
CONF_HALO = 16
SC_HALO = 8
CONV_ROWS = 16
ROUTE_TILE = 256
SLOT_WINDOW = 64
SLOT_ALIGN = 16
UNSELECTED = -(1 << 20)
V7X_VMEM_LIMIT = 56 * 1024 * 1024


def _sigmoid(x):
    return 1.0 / (1.0 + jnp.exp(-x))


def _rms(x):
    return x * lax.rsqrt(jnp.mean(x * x, axis=-1, keepdims=True) + EPS)


def _inproj_kernel(x_ref, g_ref, w_ref, u_ref, *, dc, ds):
    h = (_rms(x_ref[...]) * g_ref[...]).astype(BF16)

    def proj(c0, width):
        return jnp.dot(h, w_ref[:, c0:c0 + width], preferred_element_type=F32)

    u_ref[:, 0:dc] = proj(0, dc) * _sigmoid(proj(dc, dc))
    cg = proj(2 * dc + ds, ds)
    u_ref[:, dc:dc + ds] = cg * proj(2 * dc + 2 * ds, ds)
    u_ref[:, dc + ds:dc + 2 * ds] = proj(2 * dc, ds)


def _inproj(x2, g1, w_in, dc, ds, tm):
    n, d = x2.shape
    cols = w_in.shape[1]
    return pl.pallas_call(
        functools.partial(_inproj_kernel, dc=dc, ds=ds),
        grid=(n // tm,),
        in_specs=[
            pl.BlockSpec((tm, d), lambda i: (i, 0)),
            pl.BlockSpec((1, d), lambda i: (0, 0)),
            pl.BlockSpec((d, cols), lambda i: (0, 0), pipeline_mode=pl.Buffered(1)),
        ],
        out_specs=pl.BlockSpec((tm, dc + 2 * ds), lambda i: (i, 0)),
        out_shape=jax.ShapeDtypeStruct((n, dc + 2 * ds), F32),
        compiler_params=pltpu.CompilerParams(
            dimension_semantics=("parallel",), vmem_limit_bytes=V7X_VMEM_LIMIT),
        name="inproj",
    )(x2, g1, w_in)


def _mixer_kernel(u_ref, ap_ref, an_ref, cp_ref, cn_ref, x_ref,
                  cw_ref, cb_ref, lg_ref, lb_ref, sw_ref, sb_ref, goc_ref, gos_ref,
                  wout_ref, g2_ref, wrt_ref,
                  x1_ref, h2_ref, aff_ref,
                  aext_ref, cext_ref, y_ref, *, ts, dc, ds, nt, kc, ks):
    t = pl.program_id(1)
    keep_prev = (t > 0).astype(F32)
    keep_next = (t < nt - 1).astype(F32)
    aext_ref[0:CONF_HALO, :] = ap_ref[0] * keep_prev
    aext_ref[CONF_HALO:CONF_HALO + ts, :] = u_ref[0, :, 0:dc]
    aext_ref[CONF_HALO + ts:2 * CONF_HALO + ts, :] = an_ref[0] * keep_next
    cext_ref[0:SC_HALO, :] = cp_ref[0] * keep_prev
    cext_ref[SC_HALO:SC_HALO + ts, :] = u_ref[0, :, dc:dc + ds]
    cext_ref[SC_HALO + ts:2 * SC_HALO + ts, :] = cn_ref[0] * keep_next

    pad_c = (kc - 1) // 2
    pad_s = (ks - 1) // 2
    for c in range(ts // CONV_ROWS):
        r0 = c * CONV_ROWS
        acc = jnp.broadcast_to(cb_ref[...], (CONV_ROWS, dc))
        for k in range(kc):
            o = r0 + CONF_HALO + k - pad_c
            acc = acc + cw_ref[k:k + 1, :] * aext_ref[o:o + CONV_ROWS, :]
        mu = jnp.mean(acc, axis=-1, keepdims=True)
        cen = acc - mu
        var = jnp.mean(cen * cen, axis=-1, keepdims=True)
        a = cen * lax.rsqrt(var + EPS) * lg_ref[...] + lb_ref[...]
        a = a * _sigmoid(a)
        y_ref[r0:r0 + CONV_ROWS, 0:dc] = (_rms(a) * goc_ref[...]).astype(BF16)

        conv = jnp.broadcast_to(sb_ref[...], (CONV_ROWS, ds))
        for k in range(ks):
            o = r0 + SC_HALO + k - pad_s
            conv = conv + sw_ref[k:k + 1, :] * cext_ref[o:o + CONV_ROWS, :]
        s = u_ref[0, r0:r0 + CONV_ROWS, dc + ds:dc + 2 * ds] * conv
        y_ref[r0:r0 + CONV_ROWS, dc:dc + ds] = (_rms(s) * gos_ref[...]).astype(BF16)

    x1 = x_ref[0] + jnp.dot(y_ref[...], wout_ref[...], preferred_element_type=F32)
    x1_ref[0] = x1
    h2 = (_rms(x1) * g2_ref[...]).astype(BF16)
    h2_ref[...] = h2
    logits = lax.dot_general(wrt_ref[...], h2, (((1,), (1,)), ((), ())),
                             preferred_element_type=F32)
    ex = jnp.exp(logits - jnp.max(logits, axis=0, keepdims=True))
    aff = ex / jnp.sum(ex, axis=0, keepdims=True)
    for j in range(ts // ROUTE_TILE):
        aff_ref[j] = aff[:, j * ROUTE_TILE:(j + 1) * ROUTE_TILE]


def _mixer(u3, x3, cw, cb, lg, lb, sw, sb, goc, gos, wout, g2, wrt, ts):
    b, s, d = x3.shape
    dc = cw.shape[1]
    ds = sw.shape[1]
    kc = cw.shape[0]
    ks = sw.shape[0]
    e = wrt.shape[0]
    nt = s // ts
    n = b * s
    hb_c = ts // CONF_HALO
    hb_s = ts // SC_HALO
    col_cv = dc // ds

    def const(shape):
        return pl.BlockSpec(shape, lambda i, t: (0,) * len(shape))

    in_specs = [
        pl.BlockSpec((1, ts, dc + 2 * ds), lambda i, t: (i, t, 0)),
        pl.BlockSpec((1, CONF_HALO, dc), lambda i, t: (i, jnp.maximum(t * hb_c - 1, 0), 0)),
        pl.BlockSpec((1, CONF_HALO, dc),
                     lambda i, t: (i, jnp.minimum((t + 1) * hb_c, s // CONF_HALO - 1), 0)),
        pl.BlockSpec((1, SC_HALO, ds),
                     lambda i, t: (i, jnp.maximum(t * hb_s - 1, 0), col_cv)),
        pl.BlockSpec((1, SC_HALO, ds),
                     lambda i, t: (i, jnp.minimum((t + 1) * hb_s, s // SC_HALO - 1), col_cv)),
        pl.BlockSpec((1, ts, d), lambda i, t: (i, t, 0)),
        const((kc, dc)), const((1, dc)), const((1, dc)), const((1, dc)),
        const((ks, ds)), const((1, ds)), const((1, dc)), const((1, ds)),
        pl.BlockSpec((dc + ds, d), lambda i, t: (0, 0), pipeline_mode=pl.Buffered(1)),
        const((1, d)), const((e, d)),
    ]
    out_specs = [
        pl.BlockSpec((1, ts, d), lambda i, t: (i, t, 0)),
        pl.BlockSpec((ts, d), lambda i, t: (i * nt + t, 0)),
        pl.BlockSpec((ts // ROUTE_TILE, e, ROUTE_TILE), lambda i, t: (i * nt + t, 0, 0)),
    ]
    out_shape = [
        jax.ShapeDtypeStruct((b, s, d), F32),
        jax.ShapeDtypeStruct((n, d), BF16),
        jax.ShapeDtypeStruct((n // ROUTE_TILE, e, ROUTE_TILE), F32),
    ]
    return pl.pallas_call(
        functools.partial(_mixer_kernel, ts=ts, dc=dc, ds=ds, nt=nt, kc=kc, ks=ks),
        grid=(b, nt),
        in_specs=in_specs,
        out_specs=out_specs,
        out_shape=out_shape,
        scratch_shapes=[
            pltpu.VMEM((ts + 2 * CONF_HALO, dc), F32),
            pltpu.VMEM((ts + 2 * SC_HALO, ds), F32),
            pltpu.VMEM((ts, dc + ds), BF16),
        ],
        compiler_params=pltpu.CompilerParams(
            dimension_semantics=("parallel", "parallel"), vmem_limit_bytes=V7X_VMEM_LIMIT),
        name="mixer",
    )(u3, u3, u3, u3, u3, x3, cw, cb, lg, lb, sw, sb, goc, gos, wout, g2, wrt)


def _route_kernel(aff_ref, rel_ref, gate_ref, base_ref, *, cap, nc, e):
    capf = jnp.float32(cap)

    def bits_of(c):
        return pltpu.bitcast(aff_ref[c], I32)

    def search(i, cur):
        cand = cur | lax.shift_left(jnp.int32(1), 30 - i)

        def count(c, tot):
            return tot + (bits_of(c) >= cand).astype(F32)

        tot = lax.fori_loop(0, nc, count, jnp.zeros((e, ROUTE_TILE), F32))
        cnt = jnp.sum(tot, axis=-1, keepdims=True)
        return jnp.where(cnt >= capf, cand, cur)

    thr = lax.fori_loop(0, 31, search, jnp.zeros((e, 1), I32))

    def count_gt(c, tot):
        return tot + (bits_of(c) > thr).astype(F32)

    n_gt = jnp.sum(lax.fori_loop(0, nc, count_gt, jnp.zeros((e, ROUTE_TILE), F32)),
                   axis=-1, keepdims=True)
    need_eq = capf - n_gt

    row = lax.broadcasted_iota(I32, (ROUTE_TILE, ROUTE_TILE), 0)
    col = lax.broadcasted_iota(I32, (ROUTE_TILE, ROUTE_TILE), 1)
    tri = (row <= col).astype(BF16)

    def chunk(c, carry):
        eq_before, sel_before = carry
        a = aff_ref[c]
        b = pltpu.bitcast(a, I32)
        gt = b > thr
        eq = (b == thr)
        eqf = eq.astype(F32)
        eq_incl = jnp.dot(eqf.astype(BF16), tri, preferred_element_type=F32)
        eq_rank = eq_incl - eqf + eq_before
        sel = gt | (eq & (eq_rank < need_eq))
        self_ = sel.astype(F32)
        sel_incl = jnp.dot(self_.astype(BF16), tri, preferred_element_type=F32)
        slot = sel_incl - self_ + sel_before
        window = jnp.floor(sel_before * (1.0 / SLOT_ALIGN)) * SLOT_ALIGN
        rel_ref[c] = jnp.where(sel, (slot - window).astype(I32), UNSELECTED)
        gate_ref[c] = jnp.where(sel, a, 0.0)
        base_ref[c] = jnp.broadcast_to(sel_before, (e, 128)).astype(I32)
        return (eq_before + eq_incl[:, ROUTE_TILE - 1:ROUTE_TILE],
                sel_before + sel_incl[:, ROUTE_TILE - 1:ROUTE_TILE])

    zero = jnp.zeros((e, 1), F32)
    lax.fori_loop(0, nc, chunk, (zero, zero))


def _route(aff3, cap):
    nc, e, _ = aff3.shape
    full = lambda shape: pl.BlockSpec(shape, lambda: (0,) * len(shape))
    return pl.pallas_call(
        functools.partial(_route_kernel, cap=cap, nc=nc, e=e),
        in_specs=[full((nc, e, ROUTE_TILE))],
        out_specs=[full((nc, e, ROUTE_TILE)), full((nc, e, ROUTE_TILE)), full((nc, e, 128))],
        out_shape=[
            jax.ShapeDtypeStruct((nc, e, ROUTE_TILE), I32),
            jax.ShapeDtypeStruct((nc, e, ROUTE_TILE), F32),
            jax.ShapeDtypeStruct((nc, e, 128), I32),
        ],
        name="route",
    )(aff3)


def _align_down(slot):
    shift = SLOT_ALIGN.bit_length() - 1
    return (slot >> shift) << shift


def _window_start(base_ref, i, ex, e):
    return _align_down(base_ref[i * e + ex])


def _dispatch_kernel(base_ref, npass_ref, h2_ref, rel_ref, gate_ref,
                     xe_ref, gs_ref,
                     stage_ref, gstage_ref, carry_ref, gcarry_ref, count_ref, sem,
                     *, e, cap, nc):
    m = SLOT_WINDOW
    i = pl.program_id(0)

    @pl.when(i == 0)
    def _():
        carry_ref[...] = jnp.zeros_like(carry_ref)
        gcarry_ref[...] = jnp.zeros_like(gcarry_ref)
        count_ref[0] = 0

    def copies(buf, ex, dst_row):
        rows = pl.ds(ex * m, m)
        return (
            pltpu.make_async_copy(stage_ref.at[buf, rows], xe_ref.at[ex, pl.ds(dst_row, m)],
                                  sem.at[0]),
            pltpu.make_async_copy(gstage_ref.at[buf, rows], gs_ref.at[ex, pl.ds(dst_row, m)],
                                  sem.at[1]),
        )

    def wait_all(buf):
        for ex in range(e):
            for cp in copies(buf, ex, 0):
                cp.wait()

    h2 = h2_ref[...]
    jota = lax.broadcasted_iota(I32, (e, m, ROUTE_TILE), 1)

    def one_pass(q, _):
        done = count_ref[0]
        buf = done % 2
        relq = rel_ref[0] - q * m
        hit = relq[:, None, :] == jota
        p = hit.astype(F32)
        stage_ref[buf] = jnp.dot(p.reshape(e * m, ROUTE_TILE).astype(BF16), h2,
                                 preferred_element_type=F32).astype(BF16)
        gsel = jnp.sum(p * gate_ref[0][:, None, :], axis=-1, keepdims=True)
        gstage_ref[buf] = jnp.broadcast_to(gsel, (e, m, 128)).reshape(e * m, 128)

        for ex in range(e):
            head = pl.ds(ex * m, SLOT_ALIGN)
            wp = _window_start(base_ref, i, ex, e)
            hi = base_ref[(i + 1) * e + ex]

            @pl.when(q == 0)
            def _():
                stage_ref[buf, head, :] = (
                    stage_ref[buf, head, :].astype(F32) + carry_ref[ex]).astype(BF16)
                gstage_ref[buf, head, :] = gstage_ref[buf, head, :] + gcarry_ref[ex]
                carry_ref[ex] = jnp.zeros((SLOT_ALIGN, carry_ref.shape[-1]), F32)
                gcarry_ref[ex] = jnp.zeros((SLOT_ALIGN, 128), F32)

            off = _align_down(hi) - wp - q * m

            @pl.when((off >= 0) & (off < m))
            def _():
                tail = pl.ds(pl.multiple_of(ex * m + off, SLOT_ALIGN), SLOT_ALIGN)
                carry_ref[ex] = stage_ref[buf, tail, :].astype(F32)
                gcarry_ref[ex] = gstage_ref[buf, tail, :]

        @pl.when(done > 0)
        def _():
            wait_all(1 - buf)

        for ex in range(e):
            wp = _window_start(base_ref, i, ex, e)
            hi = base_ref[(i + 1) * e + ex]
            needed = (hi - wp > q * m) | (q == 0)
            dst = jnp.where(needed, wp + q * m, cap)
            for cp in copies(buf, ex, pl.multiple_of(dst, SLOT_ALIGN)):
                cp.start()
        count_ref[0] = done + 1
        return 0

    lax.fori_loop(0, npass_ref[i], one_pass, 0)

    @pl.when(i == nc - 1)
    def _():
        wait_all((count_ref[0] - 1) % 2)
        stage_ref[0] = jnp.zeros(stage_ref.shape[1:], BF16)
        gstage_ref[0] = jnp.zeros(gstage_ref.shape[1:], F32)
        for ex in range(e):
            for cp in copies(0, ex, cap):
                cp.start()
        wait_all(0)


def _dispatch(base, npass, h2, rel3, gate3, cap):
    nc, e, _ = rel3.shape
    d = h2.shape[1]
    m = SLOT_WINDOW
    grid_spec = pltpu.PrefetchScalarGridSpec(
        num_scalar_prefetch=2,
        grid=(nc,),
        in_specs=[
            pl.BlockSpec((ROUTE_TILE, d), lambda i, *_: (i, 0)),
            pl.BlockSpec((1, e, ROUTE_TILE), lambda i, *_: (i, 0, 0)),
            pl.BlockSpec((1, e, ROUTE_TILE), lambda i, *_: (i, 0, 0)),
        ],
        out_specs=[pl.BlockSpec(memory_space=pl.ANY), pl.BlockSpec(memory_space=pl.ANY)],
        scratch_shapes=[
            pltpu.VMEM((2, e * m, d), BF16),
            pltpu.VMEM((2, e * m, 128), F32),
            pltpu.VMEM((e, SLOT_ALIGN, d), F32),
            pltpu.VMEM((e, SLOT_ALIGN, 128), F32),
            pltpu.SMEM((1,), I32),
            pltpu.SemaphoreType.DMA((2,)),
        ],
    )
    return pl.pallas_call(
        functools.partial(_dispatch_kernel, e=e, cap=cap, nc=nc),
        grid_spec=grid_spec,
        out_shape=[
            jax.ShapeDtypeStruct((e, cap + m, d), BF16),
            jax.ShapeDtypeStruct((e, cap + m, 128), F32),
        ],
        compiler_params=pltpu.CompilerParams(
            dimension_semantics=("arbitrary",), vmem_limit_bytes=V7X_VMEM_LIMIT),
        name="dispatch",
    )(base, npass, h2, rel3, gate3)


def _ffn_kernel(xe_ref, gs_ref, wg_ref, wu_ref, wd_ref, y_ref, acc_ref, *, nf):
    f = pl.program_id(2)
    xe = xe_ref[0]
    g = jnp.dot(xe, wg_ref[0], preferred_element_type=F32)
    u = jnp.dot(xe, wu_ref[0], preferred_element_type=F32)
    hid = (g * _sigmoid(g) * u).astype(BF16)
    part = jnp.dot(hid, wd_ref[0], preferred_element_type=F32)

    @pl.when(f == 0)
    def _():
        acc_ref[...] = part

    @pl.when(f > 0)
    def _():
        acc_ref[...] += part

    @pl.when(f == nf - 1)
    def _():
        y_ref[0] = (acc_ref[...] * gs_ref[0][:, 0:1]).astype(BF16)


def _ffn(xe, gs, wg, wu, wd, cap, rt, fc):
    e, _, d = xe.shape
    dff = wg.shape[2]
    nf = dff // fc
    return pl.pallas_call(
        functools.partial(_ffn_kernel, nf=nf),
        grid=(e, cap // rt, nf),
        in_specs=[
            pl.BlockSpec((1, rt, d), lambda ex, r, f: (ex, r, 0)),
            pl.BlockSpec((1, rt, 128), lambda ex, r, f: (ex, r, 0)),
            pl.BlockSpec((1, d, fc), lambda ex, r, f: (ex, 0, f)),
            pl.BlockSpec((1, d, fc), lambda ex, r, f: (ex, 0, f)),
            pl.BlockSpec((1, fc, d), lambda ex, r, f: (ex, f, 0)),
        ],
        out_specs=pl.BlockSpec((1, rt, d), lambda ex, r, f: (ex, r, 0)),
        out_shape=jax.ShapeDtypeStruct((e, cap, d), BF16),
        scratch_shapes=[pltpu.VMEM((rt, d), F32)],
        compiler_params=pltpu.CompilerParams(
            dimension_semantics=("parallel", "parallel", "arbitrary"),
            vmem_limit_bytes=V7X_VMEM_LIMIT),
        name="ffn",
    )(xe, gs, wg, wu, wd)


def _combine_kernel(base_ref, npass_ref, x1_ref, rel_ref, gf_ref, y_ref, out_ref,
                    ystage_ref, sem, *, e, cap):
    m = SLOT_WINDOW
    i = pl.program_id(0)
    jota = lax.broadcasted_iota(I32, (e, m, ROUTE_TILE), 1)
    eota = lax.broadcasted_iota(I32, (e, 1, 1), 0)

    def one_pass(q, acc):
        shift = jnp.zeros((e, 1, 1), I32)
        cps = []
        for ex in range(e):
            wp = _window_start(base_ref, i, ex, e) + q * m
            rp = jnp.minimum(wp, cap - m)
            shift = jnp.where(eota == ex, wp - rp, shift)
            cp = pltpu.make_async_copy(
                y_ref.at[ex, pl.ds(pl.multiple_of(rp, SLOT_ALIGN), m)],
                ystage_ref.at[pl.ds(ex * m, m)], sem.at[0])
            cp.start()
            cps.append(cp)
        for cp in cps:
            cp.wait()
        relq = (rel_ref[0] - q * m)[:, None, :]
        hit = (relq >= 0) & (relq < m) & (relq + shift == jota)
        p = hit.astype(F32).reshape(e * m, ROUTE_TILE).astype(BF16)
        return acc + lax.dot_general(p, ystage_ref[...], (((0,), (0,)), ((), ())),
                                     preferred_element_type=F32)

    x2 = lax.fori_loop(0, npass_ref[i], one_pass, x1_ref[...])
    out_ref[...] = _rms(x2) * gf_ref[...]


def _combine(base, npass, x1, rel3, gf, y, cap):
    nc, e, _ = rel3.shape
    n, d = x1.shape
    m = SLOT_WINDOW
    grid_spec = pltpu.PrefetchScalarGridSpec(
        num_scalar_prefetch=2,
        grid=(nc,),
        in_specs=[
            pl.BlockSpec((ROUTE_TILE, d), lambda i, *_: (i, 0)),
            pl.BlockSpec((1, e, ROUTE_TILE), lambda i, *_: (i, 0, 0)),
            pl.BlockSpec((1, d), lambda i, *_: (0, 0)),
            pl.BlockSpec(memory_space=pl.ANY),
        ],
        out_specs=pl.BlockSpec((ROUTE_TILE, d), lambda i, *_: (i, 0)),
        scratch_shapes=[
            pltpu.VMEM((e * m, d), BF16),
            pltpu.SemaphoreType.DMA((1,)),
        ],
    )
    return pl.pallas_call(
        functools.partial(_combine_kernel, e=e, cap=cap),
        grid_spec=grid_spec,
        out_shape=jax.ShapeDtypeStruct((n, d), F32),
        compiler_params=pltpu.CompilerParams(
            dimension_semantics=("arbitrary",), vmem_limit_bytes=V7X_VMEM_LIMIT),
        name="combine",
    )(base, npass, x1, rel3, gf, y)


def _largest_tile(total, limit, multiple):
    t = min(total, limit)
    while total % t or t % multiple:
        t -= 1
    return t


def _trunk(x, w, cap_factor):
    b, s, d = x.shape
    n = b * s
    e = w["wrt"].shape[0]
    dc = w["cw"].shape[1]
    ds = w["sw"].shape[1]
    cap = cap_factor * n // e
    assert n % ROUTE_TILE == 0 and s % ROUTE_TILE == 0 and cap >= SLOT_WINDOW
    assert cap % SLOT_ALIGN == 0

    tm = _largest_tile(n, 256, 8)
    ts = _largest_tile(s, 256, ROUTE_TILE)
    u = _inproj(x.reshape(n, d), w["g1"], w["w_in"], dc, ds, tm)
    x1, h2, aff3 = _mixer(u.reshape(b, s, dc + 2 * ds), x, w["cw"], w["cb"], w["lg"], w["lb"],
                          w["sw"], w["sb"], w["goc"], w["gos"], w["w_out"], w["g2"], w["wrt"], ts)
    rel3, gate3, base3 = _route(aff3, cap)

    base = jnp.concatenate([base3[:, :, 0], jnp.full((1, e), cap, I32)], axis=0)
    rows = base[1:] - _align_down(base[:-1])
    npass = jnp.maximum((jnp.max(rows, axis=1) + SLOT_WINDOW - 1) // SLOT_WINDOW, 1).astype(I32)
    base = base.reshape(-1)

    xe, gs = _dispatch(base, npass, h2, rel3, gate3, cap)
    rt = _largest_tile(cap, 512, SLOT_ALIGN)
    fc = _largest_tile(w["wg"].shape[2], 512, 128)
    y = _ffn(xe, gs, w["wg"], w["wu"], w["wd"], cap, rt, fc)
    out = _combine(base, npass, x1.reshape(n, d), rel3, w["gf"], y, cap)
    return out.reshape(b, s, d)


def kernel(x_prompt, x_sample, g_norm1, w_in, conf_dw_w, conf_dw_b, conf_ln_g, conf_ln_b,
           sc_dw_w, sc_dw_b, g_out_conf, g_out_sc, w_out, g_norm2, w_router, w_gate, w_up,
           w_down, g_final):
    depth = w_in.shape[0]
    e = w_router.shape[2]
    cap_factor = 2
    row = lambda v: v.reshape(1, -1)
    layers = []
    for l in range(depth):
        layers.append(dict(
            g1=row(g_norm1[l]), w_in=w_in[l].astype(BF16),
            cw=conf_dw_w[l], cb=row(conf_dw_b[l]), lg=row(conf_ln_g[l]), lb=row(conf_ln_b[l]),
            sw=sc_dw_w[l], sb=row(sc_dw_b[l]), goc=row(g_out_conf[l]), gos=row(g_out_sc[l]),
            w_out=w_out[l].astype(BF16), g2=row(g_norm2[l]),
            wrt=w_router[l].T.astype(BF16),
            wg=w_gate[l].astype(BF16), wu=w_up[l].astype(BF16), wd=w_down[l].astype(BF16),
        ))
    assert depth == 1, "the final norm is fused into the last layer's combine"
    outs = []
    for x in (x_prompt, x_sample):
        w = dict(layers[0], gf=row(g_final))
        outs.append(_trunk(x, w, cap_factor))
    return tuple(outs)
```

```python
import functools

import jax
import jax.numpy as jnp
from jax import lax
from jax.experimental import pallas as pl
from jax.experimental.pallas import tpu as pltpu

F32 = jnp.float32
BF16 = jnp.bfloat16
I32 = jnp.int32

EPS = 1e-6
CONF_HALO = 16
SC_HALO = 8
SUBLANES = 8
CONV_ROWS = 32
ROUTE_TILE = 256
SLOT_WINDOW = 64
SLOT_ALIGN = 16
UNSELECTED = -(1 << 20)
V7X_VMEM_LIMIT = 56 * 1024 * 1024


def _sigmoid(x):
    return 1.0 / (1.0 + jnp.exp(-x))


def _rms(x):
    return x * lax.rsqrt(jnp.mean(x * x, axis=-1, keepdims=True) + EPS)


def _inproj_kernel(x_ref, g_ref, w_ref, u_ref, *, dc, ds):
    h = (_rms(x_ref[...]) * g_ref[...]).astype(BF16)

    def proj(c0, width):
        return jnp.dot(h, w_ref[:, c0:c0 + width], preferred_element_type=F32)

    u_ref[:, 0:dc] = proj(0, dc) * _sigmoid(proj(dc, dc))
    cg = proj(2 * dc + ds, ds)
    u_ref[:, dc:dc + ds] = cg * proj(2 * dc + 2 * ds, ds)
    u_ref[:, dc + ds:dc + 2 * ds] = proj(2 * dc, ds)


def _inproj(x2, g1, w_in, dc, ds, tm):
    n, d = x2.shape
    cols = w_in.shape[1]
    return pl.pallas_call(
        functools.partial(_inproj_kernel, dc=dc, ds=ds),
        grid=(n // tm,),
        in_specs=[
            pl.BlockSpec((tm, d), lambda i: (i, 0)),
            pl.BlockSpec((1, d), lambda i: (0, 0)),
            pl.BlockSpec((d, cols), lambda i: (0, 0), pipeline_mode=pl.Buffered(1)),
        ],
        out_specs=pl.BlockSpec((tm, dc + 2 * ds), lambda i: (i, 0)),
        out_shape=jax.ShapeDtypeStruct((n, dc + 2 * ds), F32),
        compiler_params=pltpu.CompilerParams(
            dimension_semantics=("parallel",), vmem_limit_bytes=V7X_VMEM_LIMIT),
        name="inproj",
    )(x2, g1, w_in)


def _sc_shifts(ks):
    pad = (ks - 1) // 2
    return sorted({(SC_HALO + k - pad) % SUBLANES for k in range(ks)} - {0})


def _mixer_kernel(u_ref, ap_ref, an_ref, cp_ref, cn_ref, x_ref,
                  cw_ref, cb_ref, lg_ref, lb_ref, sw_ref, sb_ref, goc_ref, gos_ref,
                  wout_ref, g2_ref, wrt_ref,
                  x1_ref, h2_ref, aff_ref,
                  aext_ref, cext_ref, ash_ref, csh_ref, y_ref, *, ts, dc, ds, nt, kc, ks):
    t = pl.program_id(1)
    keep_prev = (t > 0).astype(F32)
    keep_next = (t < nt - 1).astype(F32)
    aext_ref[0:CONF_HALO, :] = ap_ref[0] * keep_prev
    aext_ref[CONF_HALO:CONF_HALO + ts, :] = u_ref[0, :, 0:dc]
    aext_ref[CONF_HALO + ts:2 * CONF_HALO + ts, :] = an_ref[0] * keep_next
    cext_ref[0:SC_HALO, :] = cp_ref[0] * keep_prev
    cext_ref[SC_HALO:SC_HALO + ts, :] = u_ref[0, :, dc:dc + ds]
    cext_ref[SC_HALO + ts:2 * SC_HALO + ts, :] = cn_ref[0] * keep_next

    pad_c = (kc - 1) // 2
    pad_s = (ks - 1) // 2
    for r in range(1, SUBLANES):
        ash_ref[r - 1] = aext_ref[r:r + ash_ref.shape[1], :]
    sc_shifts = _sc_shifts(ks)
    for slot, r in enumerate(sc_shifts):
        csh_ref[slot] = cext_ref[r:r + csh_ref.shape[1], :]

    groups = CONV_ROWS // SUBLANES

    def grouped(v):
        return v.reshape(groups, SUBLANES, v.shape[-1])

    def shifted(ext_ref, sh_ref, o, index):
        q, r = divmod(o, SUBLANES)
        rows = pl.ds(q * SUBLANES, CONV_ROWS)
        return grouped(ext_ref[rows, :] if r == 0 else sh_ref[index(r), rows, :])

    for c in range(ts // CONV_ROWS):
        r0 = c * CONV_ROWS
        acc = jnp.broadcast_to(cb_ref[...][None], (groups, SUBLANES, dc))
        for k in range(kc):
            tap = shifted(aext_ref, ash_ref, r0 + CONF_HALO + k - pad_c, lambda r: r - 1)
            acc = acc + cw_ref[k][None] * tap
        mu = jnp.mean(acc, axis=-1, keepdims=True)
        cen = acc - mu
        var = jnp.mean(cen * cen, axis=-1, keepdims=True)
        a = cen * lax.rsqrt(var + EPS) * lg_ref[...][None] + lb_ref[...][None]
        a = a * _sigmoid(a)
        y_ref[r0:r0 + CONV_ROWS, 0:dc] = (
            _rms(a) * goc_ref[...][None]).reshape(CONV_ROWS, dc).astype(BF16)

        conv = jnp.broadcast_to(sb_ref[...][None], (groups, SUBLANES, ds))
        for k in range(ks):
            tap = shifted(cext_ref, csh_ref, r0 + SC_HALO + k - pad_s, sc_shifts.index)
            conv = conv + sw_ref[k][None] * tap
        s = grouped(u_ref[0, r0:r0 + CONV_ROWS, dc + ds:dc + 2 * ds]) * conv
        y_ref[r0:r0 + CONV_ROWS, dc:dc + ds] = (
            _rms(s) * gos_ref[...][None]).reshape(CONV_ROWS, ds).astype(BF16)

    x1 = x_ref[0] + jnp.dot(y_ref[...], wout_ref[...], preferred_element_type=F32)
    x1_ref[0] = x1
    h2 = (_rms(x1) * g2_ref[...]).astype(BF16)
    h2_ref[...] = h2
    logits = lax.dot_general(wrt_ref[...], h2, (((1,), (1,)), ((), ())),
                             preferred_element_type=F32)
    ex = jnp.exp(logits - jnp.max(logits, axis=0, keepdims=True))
    aff = ex / jnp.sum(ex, axis=0, keepdims=True)
    for j in range(ts // ROUTE_TILE):
        aff_ref[j] = aff[:, j * ROUTE_TILE:(j + 1) * ROUTE_TILE]


def _replicate_rows(v):
    return jnp.broadcast_to(v, (SUBLANES, v.shape[1]))


def _replicate_taps(w):
    return jnp.broadcast_to(w[:, None, :], (w.shape[0], SUBLANES, w.shape[1]))


def _mixer(u3, x3, cw, cb, lg, lb, sw, sb, goc, gos, wout, g2, wrt, ts):
    b, s, d = x3.shape
    dc = cw.shape[1]
    ds = sw.shape[1]
    kc = cw.shape[0]
    ks = sw.shape[0]
    e = wrt.shape[0]
    nt = s // ts
    n = b * s
    hb_c = ts // CONF_HALO
    hb_s = ts // SC_HALO
    col_cv = dc // ds

    def const(shape):
        return pl.BlockSpec(shape, lambda i, t: (0,) * len(shape))

    in_specs = [
        pl.BlockSpec((1, ts, dc + 2 * ds), lambda i, t: (i, t, 0)),
        pl.BlockSpec((1, CONF_HALO, dc), lambda i, t: (i, jnp.maximum(t * hb_c - 1, 0), 0)),
        pl.BlockSpec((1, CONF_HALO, dc),
                     lambda i, t: (i, jnp.minimum((t + 1) * hb_c, s // CONF_HALO - 1), 0)),
        pl.BlockSpec((1, SC_HALO, ds),
                     lambda i, t: (i, jnp.maximum(t * hb_s - 1, 0), col_cv)),
        pl.BlockSpec((1, SC_HALO, ds),
                     lambda i, t: (i, jnp.minimum((t + 1) * hb_s, s // SC_HALO - 1), col_cv)),
        pl.BlockSpec((1, ts, d), lambda i, t: (i, t, 0)),
        const((kc, SUBLANES, dc)), const((SUBLANES, dc)), const((SUBLANES, dc)),
        const((SUBLANES, dc)),
        const((ks, SUBLANES, ds)), const((SUBLANES, ds)), const((SUBLANES, dc)),
        const((SUBLANES, ds)),
        pl.BlockSpec((dc + ds, d), lambda i, t: (0, 0), pipeline_mode=pl.Buffered(1)),
        const((1, d)), const((e, d)),
    ]
    out_specs = [
        pl.BlockSpec((1, ts, d), lambda i, t: (i, t, 0)),
        pl.BlockSpec((ts, d), lambda i, t: (i * nt + t, 0)),
        pl.BlockSpec((ts // ROUTE_TILE, e, ROUTE_TILE), lambda i, t: (i * nt + t, 0, 0)),
    ]
    out_shape = [
        jax.ShapeDtypeStruct((b, s, d), F32),
        jax.ShapeDtypeStruct((n, d), BF16),
        jax.ShapeDtypeStruct((n // ROUTE_TILE, e, ROUTE_TILE), F32),
    ]
    return pl.pallas_call(
        functools.partial(_mixer_kernel, ts=ts, dc=dc, ds=ds, nt=nt, kc=kc, ks=ks),
        grid=(b, nt),
        in_specs=in_specs,
        out_specs=out_specs,
        out_shape=out_shape,
        scratch_shapes=[
            pltpu.VMEM((ts + 2 * CONF_HALO, dc), F32),
            pltpu.VMEM((ts + 2 * SC_HALO, ds), F32),
            pltpu.VMEM((SUBLANES - 1, ts + 2 * CONF_HALO - SUBLANES, dc), F32),
            pltpu.VMEM((len(_sc_shifts(ks)), ts + 2 * SC_HALO - SUBLANES, ds), F32),
            pltpu.VMEM((ts, dc + ds), BF16),
        ],
        compiler_params=pltpu.CompilerParams(
            dimension_semantics=("parallel", "parallel"), vmem_limit_bytes=V7X_VMEM_LIMIT),
        name="mixer",
    )(u3, u3, u3, u3, u3, x3, _replicate_taps(cw), _replicate_rows(cb), _replicate_rows(lg),
      _replicate_rows(lb), _replicate_taps(sw), _replicate_rows(sb), _replicate_rows(goc),
      _replicate_rows(gos), wout, g2, wrt)


def _route_kernel(aff_ref, rel_ref, gate_ref, base_ref, *, cap, nc, e):
    capf = jnp.float32(cap)

    def bits_of(c):
        return pltpu.bitcast(aff_ref[c], I32)

    def search(i, cur):
        cand = cur | lax.shift_left(jnp.int32(1), 30 - i)

        def count(c, tot):
            return tot + (bits_of(c) >= cand).astype(F32)

        tot = lax.fori_loop(0, nc, count, jnp.zeros((e, ROUTE_TILE), F32))
        cnt = jnp.sum(tot, axis=-1, keepdims=True)
        return jnp.where(cnt >= capf, cand, cur)

    thr = lax.fori_loop(0, 31, search, jnp.zeros((e, 1), I32))

    def count_gt(c, tot):
        return tot + (bits_of(c) > thr).astype(F32)

    n_gt = jnp.sum(lax.fori_loop(0, nc, count_gt, jnp.zeros((e, ROUTE_TILE), F32)),
                   axis=-1, keepdims=True)
    need_eq = capf - n_gt

    row = lax.broadcasted_iota(I32, (ROUTE_TILE, ROUTE_TILE), 0)
    col = lax.broadcasted_iota(I32, (ROUTE_TILE, ROUTE_TILE), 1)
    tri = (row <= col).astype(BF16)

    def chunk(c, carry):
        eq_before, sel_before = carry
        a = aff_ref[c]
        b = pltpu.bitcast(a, I32)
        gt = b > thr
        eq = (b == thr)
        eqf = eq.astype(F32)
        eq_incl = jnp.dot(eqf.astype(BF16), tri, preferred_element_type=F32)
        eq_rank = eq_incl - eqf + eq_before
        sel = gt | (eq & (eq_rank < need_eq))
        self_ = sel.astype(F32)
        sel_incl = jnp.dot(self_.astype(BF16), tri, preferred_element_type=F32)
        slot = sel_incl - self_ + sel_before
        window = jnp.floor(sel_before * (1.0 / SLOT_ALIGN)) * SLOT_ALIGN
        rel_ref[c] = jnp.where(sel, (slot - window).astype(I32), UNSELECTED)
        gate_ref[c] = jnp.where(sel, a, 0.0)
        base_ref[c] = jnp.broadcast_to(sel_before, (e, 128)).astype(I32)
        return (eq_before + eq_incl[:, ROUTE_TILE - 1:ROUTE_TILE],
                sel_before + sel_incl[:, ROUTE_TILE - 1:ROUTE_TILE])

    zero = jnp.zeros((e, 1), F32)
    lax.fori_loop(0, nc, chunk, (zero, zero))


def _route(aff3, cap):
    nc, e, _ = aff3.shape
    full = lambda shape: pl.BlockSpec(shape, lambda: (0,) * len(shape))
    return pl.pallas_call(
        functools.partial(_route_kernel, cap=cap, nc=nc, e=e),
        in_specs=[full((nc, e, ROUTE_TILE))],
        out_specs=[full((nc, e, ROUTE_TILE)), full((nc, e, ROUTE_TILE)), full((nc, e, 128))],
        out_shape=[
            jax.ShapeDtypeStruct((nc, e, ROUTE_TILE), I32),
            jax.ShapeDtypeStruct((nc, e, ROUTE_TILE), F32),
            jax.ShapeDtypeStruct((nc, e, 128), I32),
        ],
        name="route",
    )(aff3)


def _align_down(slot):
    shift = SLOT_ALIGN.bit_length() - 1
    return (slot >> shift) << shift


def _window_start(base_ref, i, ex, e):
    return _align_down(base_ref[i * e + ex])


def _dispatch_kernel(base_ref, npass_ref, h2_ref, rel_ref, gate_ref,
                     xe_ref, gs_ref,
                     stage_ref, gstage_ref, carry_ref, gcarry_ref, count_ref, sem,
                     *, e, cap, nc):
    m = SLOT_WINDOW
    i = pl.program_id(0)

    @pl.when(i == 0)
    def _():
        carry_ref[...] = jnp.zeros_like(carry_ref)
        gcarry_ref[...] = jnp.zeros_like(gcarry_ref)
        count_ref[0] = 0

    def copies(buf, ex, dst_row):
        rows = pl.ds(ex * m, m)
        return (
            pltpu.make_async_copy(stage_ref.at[buf, rows], xe_ref.at[ex, pl.ds(dst_row, m)],
                                  sem.at[0]),
            pltpu.make_async_copy(gstage_ref.at[buf, rows], gs_ref.at[ex, pl.ds(dst_row, m)],
                                  sem.at[1]),
        )

    def wait_all(buf):
        for ex in range(e):
            for cp in copies(buf, ex, 0):
                cp.wait()

    h2 = h2_ref[...]
    jota = lax.broadcasted_iota(I32, (e, m, ROUTE_TILE), 1)

    def one_pass(q, _):
        done = count_ref[0]
        buf = done % 2
        relq = rel_ref[0] - q * m
        hit = relq[:, None, :] == jota
        p = hit.astype(F32)
        stage_ref[buf] = jnp.dot(p.reshape(e * m, ROUTE_TILE).astype(BF16), h2,
                                 preferred_element_type=F32).astype(BF16)
        gsel = jnp.sum(p * gate_ref[0][:, None, :], axis=-1, keepdims=True)
        gstage_ref[buf] = jnp.broadcast_to(gsel, (e, m, 128)).reshape(e * m, 128)

        for ex in range(e):
            head = pl.ds(ex * m, SLOT_ALIGN)
            wp = _window_start(base_ref, i, ex, e)
            hi = base_ref[(i + 1) * e + ex]

            @pl.when(q == 0)
            def _():
                stage_ref[buf, head, :] = (
                    stage_ref[buf, head, :].astype(F32) + carry_ref[ex]).astype(BF16)
                gstage_ref[buf, head, :] = gstage_ref[buf, head, :] + gcarry_ref[ex]
                carry_ref[ex] = jnp.zeros((SLOT_ALIGN, carry_ref.shape[-1]), F32)
                gcarry_ref[ex] = jnp.zeros((SLOT_ALIGN, 128), F32)

            off = _align_down(hi) - wp - q * m

            @pl.when((off >= 0) & (off < m))
            def _():
                tail = pl.ds(pl.multiple_of(ex * m + off, SLOT_ALIGN), SLOT_ALIGN)
                carry_ref[ex] = stage_ref[buf, tail, :].astype(F32)
                gcarry_ref[ex] = gstage_ref[buf, tail, :]

        @pl.when(done > 0)
        def _():
            wait_all(1 - buf)

        for ex in range(e):
            wp = _window_start(base_ref, i, ex, e)
            hi = base_ref[(i + 1) * e + ex]
            needed = (hi - wp > q * m) | (q == 0)
            dst = jnp.where(needed, wp + q * m, cap)
            for cp in copies(buf, ex, pl.multiple_of(dst, SLOT_ALIGN)):
                cp.start()
        count_ref[0] = done + 1
        return 0

    lax.fori_loop(0, npass_ref[i], one_pass, 0)

    @pl.when(i == nc - 1)
    def _():
        wait_all((count_ref[0] - 1) % 2)
        stage_ref[0] = jnp.zeros(stage_ref.shape[1:], BF16)
        gstage_ref[0] = jnp.zeros(gstage_ref.shape[1:], F32)
        for ex in range(e):
            for cp in copies(0, ex, cap):
                cp.start()
        wait_all(0)


def _dispatch(base, npass, h2, rel3, gate3, cap):
    nc, e, _ = rel3.shape
    d = h2.shape[1]
    m = SLOT_WINDOW
    grid_spec = pltpu.PrefetchScalarGridSpec(
        num_scalar_prefetch=2,
        grid=(nc,),
        in_specs=[
            pl.BlockSpec((ROUTE_TILE, d), lambda i, *_: (i, 0)),
            pl.BlockSpec((1, e, ROUTE_TILE), lambda i, *_: (i, 0, 0)),
            pl.BlockSpec((1, e, ROUTE_TILE), lambda i, *_: (i, 0, 0)),
        ],
        out_specs=[pl.BlockSpec(memory_space=pl.ANY), pl.BlockSpec(memory_space=pl.ANY)],
        scratch_shapes=[
            pltpu.VMEM((2, e * m, d), BF16),
            pltpu.VMEM((2, e * m, 128), F32),
            pltpu.VMEM((e, SLOT_ALIGN, d), F32),
            pltpu.VMEM((e, SLOT_ALIGN, 128), F32),
            pltpu.SMEM((1,), I32),
            pltpu.SemaphoreType.DMA((2,)),
        ],
    )
    return pl.pallas_call(
        functools.partial(_dispatch_kernel, e=e, cap=cap, nc=nc),
        grid_spec=grid_spec,
        out_shape=[
            jax.ShapeDtypeStruct((e, cap + m, d), BF16),
            jax.ShapeDtypeStruct((e, cap + m, 128), F32),
        ],
        compiler_params=pltpu.CompilerParams(
            dimension_semantics=("arbitrary",), vmem_limit_bytes=V7X_VMEM_LIMIT),
        name="dispatch",
    )(base, npass, h2, rel3, gate3)


def _ffn_kernel(xe_ref, gs_ref, wg_ref, wu_ref, wd_ref, y_ref, acc_ref, *, nf):
    f = pl.program_id(2)
    xe = xe_ref[0]
    g = jnp.dot(xe, wg_ref[0], preferred_element_type=F32)
    u = jnp.dot(xe, wu_ref[0], preferred_element_type=F32)
    hid = (g * _sigmoid(g) * u).astype(BF16)
    part = jnp.dot(hid, wd_ref[0], preferred_element_type=F32)

    @pl.when(f == 0)
    def _():
        acc_ref[...] = part

    @pl.when(f > 0)
    def _():
        acc_ref[...] += part

    @pl.when(f == nf - 1)
    def _():
        y_ref[0] = (acc_ref[...] * gs_ref[0][:, 0:1]).astype(BF16)


def _ffn(xe, gs, wg, wu, wd, cap, rt, fc):
    e, _, d = xe.shape
    dff = wg.shape[2]
    nf = dff // fc
    return pl.pallas_call(
        functools.partial(_ffn_kernel, nf=nf),
        grid=(e, cap // rt, nf),
        in_specs=[
            pl.BlockSpec((1, rt, d), lambda ex, r, f: (ex, r, 0)),
            pl.BlockSpec((1, rt, 128), lambda ex, r, f: (ex, r, 0)),
            pl.BlockSpec((1, d, fc), lambda ex, r, f: (ex, 0, f)),
            pl.BlockSpec((1, d, fc), lambda ex, r, f: (ex, 0, f)),
            pl.BlockSpec((1, fc, d), lambda ex, r, f: (ex, f, 0)),
        ],
        out_specs=pl.BlockSpec((1, rt, d), lambda ex, r, f: (ex, r, 0)),
        out_shape=jax.ShapeDtypeStruct((e, cap, d), BF16),
        scratch_shapes=[pltpu.VMEM((rt, d), F32)],
        compiler_params=pltpu.CompilerParams(
            dimension_semantics=("parallel", "parallel", "arbitrary"),
            vmem_limit_bytes=V7X_VMEM_LIMIT),
        name="ffn",
    )(xe, gs, wg, wu, wd)


def _combine_kernel(base_ref, npass_ref, x1_ref, rel_ref, gf_ref, y_ref, out_ref,
                    ystage_ref, sem, *, e, cap, nc):
    m = SLOT_WINDOW
    i = pl.program_id(0)
    buf = i % 2
    jota = lax.broadcasted_iota(I32, (e, m, ROUTE_TILE), 1)
    eota = lax.broadcasted_iota(I32, (e, 1, 1), 0)

    def windows(tile, q, into):
        shift = jnp.zeros((e, 1, 1), I32)
        cps = []
        for ex in range(e):
            wp = _window_start(base_ref, tile, ex, e) + q * m
            rp = jnp.minimum(wp, cap - m)
            shift = jnp.where(eota == ex, wp - rp, shift)
            cps.append(pltpu.make_async_copy(
                y_ref.at[ex, pl.ds(pl.multiple_of(rp, SLOT_ALIGN), m)],
                ystage_ref.at[into, pl.ds(ex * m, m)], sem.at[into]))
        return cps, shift

    def expand(q, shift, acc):
        relq = (rel_ref[0] - q * m)[:, None, :]
        hit = (relq >= 0) & (relq < m) & (relq + shift == jota)
        p = hit.astype(F32).reshape(e * m, ROUTE_TILE).astype(BF16)
        return acc + lax.dot_general(p, ystage_ref[buf], (((0,), (0,)), ((), ())),
                                     preferred_element_type=F32)

    @pl.when(i == 0)
    def _():
        for cp in windows(0, 0, 0)[0]:
            cp.start()

    @pl.when(i + 1 < nc)
    def _():
        for cp in windows(i + 1, 0, 1 - buf)[0]:
            cp.start()

    cps, shift = windows(i, 0, buf)
    for cp in cps:
        cp.wait()
    acc = expand(0, shift, x1_ref[...])

    def extra_pass(q, acc):
        cps, shift = windows(i, q, buf)
        for cp in cps:
            cp.start()
        for cp in cps:
            cp.wait()
        return expand(q, shift, acc)

    x2 = lax.fori_loop(1, npass_ref[i], extra_pass, acc)
    out_ref[...] = _rms(x2) * gf_ref[...]


def _combine(base, npass, x1, rel3, gf, y, cap):
    nc, e, _ = rel3.shape
    n, d = x1.shape
    m = SLOT_WINDOW
    grid_spec = pltpu.PrefetchScalarGridSpec(
        num_scalar_prefetch=2,
        grid=(nc,),
        in_specs=[
            pl.BlockSpec((ROUTE_TILE, d), lambda i, *_: (i, 0)),
            pl.BlockSpec((1, e, ROUTE_TILE), lambda i, *_: (i, 0, 0)),
            pl.BlockSpec((1, d), lambda i, *_: (0, 0)),
            pl.BlockSpec(memory_space=pl.ANY),
        ],
        out_specs=pl.BlockSpec((ROUTE_TILE, d), lambda i, *_: (i, 0)),
        scratch_shapes=[
            pltpu.VMEM((2, e * m, d), BF16),
            pltpu.SemaphoreType.DMA((2,)),
        ],
    )
    return pl.pallas_call(
        functools.partial(_combine_kernel, e=e, cap=cap, nc=nc),
        grid_spec=grid_spec,
        out_shape=jax.ShapeDtypeStruct((n, d), F32),
        compiler_params=pltpu.CompilerParams(
            dimension_semantics=("arbitrary",), vmem_limit_bytes=V7X_VMEM_LIMIT),
        name="combine",
    )(base, npass, x1, rel3, gf, y)


def _largest_tile(total, limit, multiple):
    t = min(total, limit)
    while total % t or t % multiple:
        t -= 1
    return t


def _trunk(x, w, cap_factor):
    b, s, d = x.shape
    n = b * s
    e = w["wrt"].shape[0]
    dc = w["cw"].shape[1]
    ds = w["sw"].shape[1]
    cap = cap_factor * n // e
    assert n % ROUTE_TILE == 0 and s % ROUTE_TILE == 0 and cap >= SLOT_WINDOW
    assert cap % SLOT_ALIGN == 0

    tm = _largest_tile(n, 256, 8)
    ts = _largest_tile(s, 256, ROUTE_TILE)
    u = _inproj(x.reshape(n, d), w["g1"], w["w_in"], dc, ds, tm)
    x1, h2, aff3 = _mixer(u.reshape(b, s, dc + 2 * ds), x, w["cw"], w["cb"], w["lg"], w["lb"],
                          w["sw"], w["sb"], w["goc"], w["gos"], w["w_out"], w["g2"], w["wrt"], ts)
    rel3, gate3, base3 = _route(aff3, cap)

    base = jnp.concatenate([base3[:, :, 0], jnp.full((1, e), cap, I32)], axis=0)
    rows = base[1:] - _align_down(base[:-1])
    npass = jnp.maximum((jnp.max(rows, axis=1) + SLOT_WINDOW - 1) // SLOT_WINDOW, 1).astype(I32)
    base = base.reshape(-1)

    xe, gs = _dispatch(base, npass, h2, rel3, gate3, cap)
    rt = _largest_tile(cap, 512, SLOT_ALIGN)
    fc = _largest_tile(w["wg"].shape[2], 512, 128)
    y = _ffn(xe, gs, w["wg"], w["wu"], w["wd"], cap, rt, fc)
    out = _combine(base, npass, x1.reshape(n, d), rel3, w["gf"], y, cap)
    return out.reshape(b, s, d)


def kernel(x_prompt, x_sample, g_norm1, w_in, conf_dw_w, conf_dw_b, conf_ln_g, conf_ln_b,
           sc_dw_w, sc_dw_b, g_out_conf, g_out_sc, w_out, g_norm2, w_router, w_gate, w_up,
           w_down, g_final):
    depth = w_in.shape[0]
    e = w_router.shape[2]
    cap_factor = 2
    row = lambda v: v.reshape(1, -1)
    layers = []
    for l in range(depth):
        layers.append(dict(
            g1=row(g_norm1[l]), w_in=w_in[l].astype(BF16),
            cw=conf_dw_w[l], cb=row(conf_dw_b[l]), lg=row(conf_ln_g[l]), lb=row(conf_ln_b[l]),
            sw=sc_dw_w[l], sb=row(sc_dw_b[l]), goc=row(g_out_conf[l]), gos=row(g_out_sc[l]),
            w_out=w_out[l].astype(BF16), g2=row(g_norm2[l]),
            wrt=w_router[l].T.astype(BF16),
            wg=w_gate[l].astype(BF16), wu=w_up[l].astype(BF16), wd=w_down[l].astype(BF16),
        ))
    assert depth == 1, "the final norm is fused into the last layer's combine"
    outs = []
    for x in (x_prompt, x_sample):
        w = dict(layers[0], gf=row(g_final))
        outs.append(_trunk(x, w, cap_factor))
    return tuple(outs)
```

```python
import functools

import jax
import jax.numpy as jnp
from jax import lax
from jax.experimental import pallas as pl
from jax.experimental.pallas import tpu as pltpu

F32 = jnp.float32
BF16 = jnp.bfloat16
I32 = jnp.int32

EPS = 1e-6
CONF_HALO = 16
SC_HALO = 8
SUBLANES = 8
CONV_ROWS = 32
ROUTE_TILE = 256
SLOT_WINDOW = 64
SLOT_ALIGN = 16
UNSELECTED = -(1 << 20)
V7X_VMEM_LIMIT = 56 * 1024 * 1024


def _sigmoid(x):
    return 1.0 / (1.0 + jnp.exp(-x))


def _rms(x):
    return x * lax.rsqrt(jnp.mean(x * x, axis=-1, keepdims=True) + EPS)


def _inproj_kernel(x_ref, g_ref, w_ref, u_ref, *, dc, ds):
    h = (_rms(x_ref[...]) * g_ref[...]).astype(BF16)

    def proj(c0, width):
        return jnp.dot(h, w_ref[:, c0:c0 + width], preferred_element_type=F32)

    u_ref[:, 0:dc] = proj(0, dc) * _sigmoid(proj(dc, dc))
    cg = proj(2 * dc + ds, ds)
    u_ref[:, dc:dc + ds] = cg * proj(2 * dc + 2 * ds, ds)
    u_ref[:, dc + ds:dc + 2 * ds] = proj(2 * dc, ds)


def _inproj(x2, g1, w_in, dc, ds, tm):
    n, d = x2.shape
    cols = w_in.shape[1]
    return pl.pallas_call(
        functools.partial(_inproj_kernel, dc=dc, ds=ds),
        grid=(n // tm,),
        in_specs=[
            pl.BlockSpec((tm, d), lambda i: (i, 0)),
            pl.BlockSpec((1, d), lambda i: (0, 0)),
            pl.BlockSpec((d, cols), lambda i: (0, 0), pipeline_mode=pl.Buffered(1)),
        ],
        out_specs=pl.BlockSpec((tm, dc + 2 * ds), lambda i: (i, 0)),
        out_shape=jax.ShapeDtypeStruct((n, dc + 2 * ds), F32),
        compiler_params=pltpu.CompilerParams(
            dimension_semantics=("parallel",), vmem_limit_bytes=V7X_VMEM_LIMIT),
        name="inproj",
    )(x2, g1, w_in)


def _sc_shifts(ks):
    pad = (ks - 1) // 2
    return sorted({(SC_HALO + k - pad) % SUBLANES for k in range(ks)} - {0})


def _mixer_kernel(u_ref, ap_ref, an_ref, cp_ref, cn_ref, x_ref,
                  cw_ref, cb_ref, lg_ref, lb_ref, sw_ref, sb_ref, goc_ref, gos_ref,
                  wout_ref, g2_ref, wrt_ref,
                  x1_ref, h2_ref, aff_ref,
                  aext_ref, cext_ref, ash_ref, csh_ref, y_ref, *, ts, dc, ds, nt, kc, ks):
    t = pl.program_id(1)
    keep_prev = (t > 0).astype(F32)
    keep_next = (t < nt - 1).astype(F32)
    aext_ref[0:CONF_HALO, :] = ap_ref[0] * keep_prev
    aext_ref[CONF_HALO:CONF_HALO + ts, :] = u_ref[0, :, 0:dc]
    aext_ref[CONF_HALO + ts:2 * CONF_HALO + ts, :] = an_ref[0] * keep_next
    cext_ref[0:SC_HALO, :] = cp_ref[0] * keep_prev
    cext_ref[SC_HALO:SC_HALO + ts, :] = u_ref[0, :, dc:dc + ds]
    cext_ref[SC_HALO + ts:2 * SC_HALO + ts, :] = cn_ref[0] * keep_next

    pad_c = (kc - 1) // 2
    pad_s = (ks - 1) // 2
    for r in range(1, SUBLANES):
        ash_ref[r - 1] = aext_ref[r:r + ash_ref.shape[1], :]
    sc_shifts = _sc_shifts(ks)
    for slot, r in enumerate(sc_shifts):
        csh_ref[slot] = cext_ref[r:r + csh_ref.shape[1], :]

    groups = CONV_ROWS // SUBLANES

    def grouped(v):
        return v.reshape(groups, SUBLANES, v.shape[-1])

    def shifted(ext_ref, sh_ref, o, index):
        q, r = divmod(o, SUBLANES)
        rows = pl.ds(q * SUBLANES, CONV_ROWS)
        return grouped(ext_ref[rows, :] if r == 0 else sh_ref[index(r), rows, :])

    for c in range(ts // CONV_ROWS):
        r0 = c * CONV_ROWS
        acc = jnp.broadcast_to(cb_ref[...][None], (groups, SUBLANES, dc))
        for k in range(kc):
            tap = shifted(aext_ref, ash_ref, r0 + CONF_HALO + k - pad_c, lambda r: r - 1)
            acc = acc + cw_ref[k][None] * tap
        mu = jnp.mean(acc, axis=-1, keepdims=True)
        cen = acc - mu
        var = jnp.mean(cen * cen, axis=-1, keepdims=True)
        a = cen * lax.rsqrt(var + EPS) * lg_ref[...][None] + lb_ref[...][None]
        a = a * _sigmoid(a)
        y_ref[r0:r0 + CONV_ROWS, 0:dc] = (
            _rms(a) * goc_ref[...][None]).reshape(CONV_ROWS, dc).astype(BF16)

        conv = jnp.broadcast_to(sb_ref[...][None], (groups, SUBLANES, ds))
        for k in range(ks):
            tap = shifted(cext_ref, csh_ref, r0 + SC_HALO + k - pad_s, sc_shifts.index)
            conv = conv + sw_ref[k][None] * tap
        s = grouped(u_ref[0, r0:r0 + CONV_ROWS, dc + ds:dc + 2 * ds]) * conv
        y_ref[r0:r0 + CONV_ROWS, dc:dc + ds] = (
            _rms(s) * gos_ref[...][None]).reshape(CONV_ROWS, ds).astype(BF16)

    x1 = x_ref[0] + jnp.dot(y_ref[...], wout_ref[...], preferred_element_type=F32)
    x1_ref[0] = x1
    h2 = (_rms(x1) * g2_ref[...]).astype(BF16)
    h2_ref[...] = h2
    logits = lax.dot_general(wrt_ref[...], h2, (((1,), (1,)), ((), ())),
                             preferred_element_type=F32)
    ex = jnp.exp(logits - jnp.max(logits, axis=0, keepdims=True))
    aff = ex / jnp.sum(ex, axis=0, keepdims=True)
    for j in range(ts // ROUTE_TILE):
        aff_ref[j] = aff[:, j * ROUTE_TILE:(j + 1) * ROUTE_TILE]


def _replicate_rows(v):
    return jnp.broadcast_to(v, (SUBLANES, v.shape[1]))


def _replicate_taps(w):
    return jnp.broadcast_to(w[:, None, :], (w.shape[0], SUBLANES, w.shape[1]))


def _mixer(u3, x3, cw, cb, lg, lb, sw, sb, goc, gos, wout, g2, wrt, ts):
    b, s, d = x3.shape
    dc = cw.shape[1]
    ds = sw.shape[1]
    kc = cw.shape[0]
    ks = sw.shape[0]
    e = wrt.shape[0]
    nt = s // ts
    n = b * s
    hb_c = ts // CONF_HALO
    hb_s = ts // SC_HALO
    col_cv = dc // ds

    def const(shape):
        return pl.BlockSpec(shape, lambda i, t: (0,) * len(shape))

    in_specs = [
        pl.BlockSpec((1, ts, dc + 2 * ds), lambda i, t: (i, t, 0)),
        pl.BlockSpec((1, CONF_HALO, dc), lambda i, t: (i, jnp.maximum(t * hb_c - 1, 0), 0)),
        pl.BlockSpec((1, CONF_HALO, dc),
                     lambda i, t: (i, jnp.minimum((t + 1) * hb_c, s // CONF_HALO - 1), 0)),
        pl.BlockSpec((1, SC_HALO, ds),
                     lambda i, t: (i, jnp.maximum(t * hb_s - 1, 0), col_cv)),
        pl.BlockSpec((1, SC_HALO, ds),
                     lambda i, t: (i, jnp.minimum((t + 1) * hb_s, s // SC_HALO - 1), col_cv)),
        pl.BlockSpec((1, ts, d), lambda i, t: (i, t, 0)),
        const((kc, SUBLANES, dc)), const((SUBLANES, dc)), const((SUBLANES, dc)),
        const((SUBLANES, dc)),
        const((ks, SUBLANES, ds)), const((SUBLANES, ds)), const((SUBLANES, dc)),
        const((SUBLANES, ds)),
        pl.BlockSpec((dc + ds, d), lambda i, t: (0, 0), pipeline_mode=pl.Buffered(1)),
        const((1, d)), const((e, d)),
    ]
    out_specs = [
        pl.BlockSpec((1, ts, d), lambda i, t: (i, t, 0)),
        pl.BlockSpec((ts, d), lambda i, t: (i * nt + t, 0)),
        pl.BlockSpec((ts // ROUTE_TILE, e, ROUTE_TILE), lambda i, t: (i * nt + t, 0, 0)),
    ]
    out_shape = [
        jax.ShapeDtypeStruct((b, s, d), F32),
        jax.ShapeDtypeStruct((n, d), BF16),
        jax.ShapeDtypeStruct((n // ROUTE_TILE, e, ROUTE_TILE), F32),
    ]
    return pl.pallas_call(
        functools.partial(_mixer_kernel, ts=ts, dc=dc, ds=ds, nt=nt, kc=kc, ks=ks),
        grid=(b, nt),
        in_specs=in_specs,
        out_specs=out_specs,
        out_shape=out_shape,
        scratch_shapes=[
            pltpu.VMEM((ts + 2 * CONF_HALO, dc), F32),
            pltpu.VMEM((ts + 2 * SC_HALO, ds), F32),
            pltpu.VMEM((SUBLANES - 1, ts + 2 * CONF_HALO - SUBLANES, dc), F32),
            pltpu.VMEM((len(_sc_shifts(ks)), ts + 2 * SC_HALO - SUBLANES, ds), F32),
            pltpu.VMEM((ts, dc + ds), BF16),
        ],
        compiler_params=pltpu.CompilerParams(
            dimension_semantics=("parallel", "parallel"), vmem_limit_bytes=V7X_VMEM_LIMIT),
        name="mixer",
    )(u3, u3, u3, u3, u3, x3, _replicate_taps(cw), _replicate_rows(cb), _replicate_rows(lg),
      _replicate_rows(lb), _replicate_taps(sw), _replicate_rows(sb), _replicate_rows(goc),
      _replicate_rows(gos), wout, g2, wrt)


def _route_kernel(aff_ref, rel_ref, gate_ref, base_ref, *, cap, nc, e):
    capf = jnp.float32(cap)

    def bits_of(c):
        return pltpu.bitcast(aff_ref[c], I32)

    def search(i, cur):
        cand = cur | lax.shift_left(jnp.int32(1), 30 - i)

        def count(c, tot):
            return tot + (bits_of(c) >= cand).astype(F32)

        tot = lax.fori_loop(0, nc, count, jnp.zeros((e, ROUTE_TILE), F32))
        cnt = jnp.sum(tot, axis=-1, keepdims=True)
        return jnp.where(cnt >= capf, cand, cur)

    thr = lax.fori_loop(0, 31, search, jnp.zeros((e, 1), I32))

    def count_gt(c, tot):
        return tot + (bits_of(c) > thr).astype(F32)

    n_gt = jnp.sum(lax.fori_loop(0, nc, count_gt, jnp.zeros((e, ROUTE_TILE), F32)),
                   axis=-1, keepdims=True)
    need_eq = capf - n_gt

    row = lax.broadcasted_iota(I32, (ROUTE_TILE, ROUTE_TILE), 0)
    col = lax.broadcasted_iota(I32, (ROUTE_TILE, ROUTE_TILE), 1)
    tri = (row <= col).astype(BF16)

    def chunk(c, carry):
        eq_before, sel_before = carry
        a = aff_ref[c]
        b = pltpu.bitcast(a, I32)
        gt = b > thr
        eq = (b == thr)
        eqf = eq.astype(F32)
        eq_incl = jnp.dot(eqf.astype(BF16), tri, preferred_element_type=F32)
        eq_rank = eq_incl - eqf + eq_before
        sel = gt | (eq & (eq_rank < need_eq))
        self_ = sel.astype(F32)
        sel_incl = jnp.dot(self_.astype(BF16), tri, preferred_element_type=F32)
        slot = sel_incl - self_ + sel_before
        window = jnp.floor(sel_before * (1.0 / SLOT_ALIGN)) * SLOT_ALIGN
        rel_ref[c] = jnp.where(sel, (slot - window).astype(I32), UNSELECTED)
        gate_ref[c] = jnp.where(sel, a, 0.0)
        base_ref[c] = jnp.broadcast_to(sel_before, (e, 128)).astype(I32)
        return (eq_before + eq_incl[:, ROUTE_TILE - 1:ROUTE_TILE],
                sel_before + sel_incl[:, ROUTE_TILE - 1:ROUTE_TILE])

    zero = jnp.zeros((e, 1), F32)
    lax.fori_loop(0, nc, chunk, (zero, zero))


def _route(aff3, cap):
    nc, e, _ = aff3.shape
    full = lambda shape: pl.BlockSpec(shape, lambda: (0,) * len(shape))
    return pl.pallas_call(
        functools.partial(_route_kernel, cap=cap, nc=nc, e=e),
        in_specs=[full((nc, e, ROUTE_TILE))],
        out_specs=[full((nc, e, ROUTE_TILE)), full((nc, e, ROUTE_TILE)), full((nc, e, 128))],
        out_shape=[
            jax.ShapeDtypeStruct((nc, e, ROUTE_TILE), I32),
            jax.ShapeDtypeStruct((nc, e, ROUTE_TILE), F32),
            jax.ShapeDtypeStruct((nc, e, 128), I32),
        ],
        name="route",
    )(aff3)


def _align_down(slot):
    shift = SLOT_ALIGN.bit_length() - 1
    return (slot >> shift) << shift


def _window_start(base_ref, i, ex, e):
    return _align_down(base_ref[i * e + ex])


def _dispatch_kernel(base_ref, npass_ref, h2_ref, rel_ref, gate_ref,
                     xe_ref, gs_ref,
                     stage_ref, gstage_ref, carry_ref, gcarry_ref, count_ref, sem,
                     *, e, cap, nc):
    m = SLOT_WINDOW
    i = pl.program_id(0)

    @pl.when(i == 0)
    def _():
        carry_ref[...] = jnp.zeros_like(carry_ref)
        gcarry_ref[...] = jnp.zeros_like(gcarry_ref)
        count_ref[0] = 0

    def copies(buf, ex, dst_row):
        rows = pl.ds(ex * m, m)
        return (
            pltpu.make_async_copy(stage_ref.at[buf, rows], xe_ref.at[ex, pl.ds(dst_row, m)],
                                  sem.at[0]),
            pltpu.make_async_copy(gstage_ref.at[buf, rows], gs_ref.at[ex, pl.ds(dst_row, m)],
                                  sem.at[1]),
        )

    def wait_all(buf):
        for ex in range(e):
            for cp in copies(buf, ex, 0):
                cp.wait()

    h2 = h2_ref[...]
    jota = lax.broadcasted_iota(I32, (e, m, ROUTE_TILE), 1)

    def one_pass(q, _):
        done = count_ref[0]
        buf = done % 2
        relq = rel_ref[0] - q * m
        hit = relq[:, None, :] == jota
        p = hit.astype(F32)
        stage_ref[buf] = jnp.dot(p.reshape(e * m, ROUTE_TILE).astype(BF16), h2,
                                 preferred_element_type=F32).astype(BF16)
        gsel = jnp.sum(p * gate_ref[0][:, None, :], axis=-1, keepdims=True)
        gstage_ref[buf] = jnp.broadcast_to(gsel, (e, m, 128)).reshape(e * m, 128)

        for ex in range(e):
            head = pl.ds(ex * m, SLOT_ALIGN)
            wp = _window_start(base_ref, i, ex, e)
            hi = base_ref[(i + 1) * e + ex]

            @pl.when(q == 0)
            def _():
                stage_ref[buf, head, :] = (
                    stage_ref[buf, head, :].astype(F32) + carry_ref[ex]).astype(BF16)
                gstage_ref[buf, head, :] = gstage_ref[buf, head, :] + gcarry_ref[ex]
                carry_ref[ex] = jnp.zeros((SLOT_ALIGN, carry_ref.shape[-1]), F32)
                gcarry_ref[ex] = jnp.zeros((SLOT_ALIGN, 128), F32)

            off = _align_down(hi) - wp - q * m

            @pl.when((off >= 0) & (off < m))
            def _():
                tail = pl.ds(pl.multiple_of(ex * m + off, SLOT_ALIGN), SLOT_ALIGN)
                carry_ref[ex] = stage_ref[buf, tail, :].astype(F32)
                gcarry_ref[ex] = gstage_ref[buf, tail, :]

        @pl.when(done > 0)
        def _():
            wait_all(1 - buf)

        for ex in range(e):
            wp = _window_start(base_ref, i, ex, e)
            hi = base_ref[(i + 1) * e + ex]
            needed = (hi - wp > q * m) | (q == 0)
            dst = jnp.where(needed, wp + q * m, cap)
            for cp in copies(buf, ex, pl.multiple_of(dst, SLOT_ALIGN)):
                cp.start()
        count_ref[0] = done + 1
        return 0

    lax.fori_loop(0, npass_ref[i], one_pass, 0)

    @pl.when(i == nc - 1)
    def _():
        wait_all((count_ref[0] - 1) % 2)
        stage_ref[0] = jnp.zeros(stage_ref.shape[1:], BF16)
        gstage_ref[0] = jnp.zeros(gstage_ref.shape[1:], F32)
        for ex in range(e):
            for cp in copies(0, ex, cap):
                cp.start()
        wait_all(0)


def _dispatch(base, npass, h2, rel3, gate3, cap):
    nc, e, _ = rel3.shape
    d = h2.shape[1]
    m = SLOT_WINDOW
    grid_spec = pltpu.PrefetchScalarGridSpec(
        num_scalar_prefetch=2,
        grid=(nc,),
        in_specs=[
            pl.BlockSpec((ROUTE_TILE, d), lambda i, *_: (i, 0)),
            pl.BlockSpec((1, e, ROUTE_TILE), lambda i, *_: (i, 0, 0)),
            pl.BlockSpec((1, e, ROUTE_TILE), lambda i, *_: (i, 0, 0)),
        ],
        out_specs=[pl.BlockSpec(memory_space=pl.ANY), pl.BlockSpec(memory_space=pl.ANY)],
        scratch_shapes=[
            pltpu.VMEM((2, e * m, d), BF16),
            pltpu.VMEM((2, e * m, 128), F32),
            pltpu.VMEM((e, SLOT_ALIGN, d), F32),
            pltpu.VMEM((e, SLOT_ALIGN, 128), F32),
            pltpu.SMEM((1,), I32),
            pltpu.SemaphoreType.DMA((2,)),
        ],
    )
    return pl.pallas_call(
        functools.partial(_dispatch_kernel, e=e, cap=cap, nc=nc),
        grid_spec=grid_spec,
        out_shape=[
            jax.ShapeDtypeStruct((e, cap + m, d), BF16),
            jax.ShapeDtypeStruct((e, cap + m, 128), F32),
        ],
        compiler_params=pltpu.CompilerParams(
            dimension_semantics=("arbitrary",), vmem_limit_bytes=V7X_VMEM_LIMIT),
        name="dispatch",
    )(base, npass, h2, rel3, gate3)


def _ffn_kernel(xe_ref, gs_ref, wg_ref, wu_ref, wd_ref, y_ref, hid_ref, *, nf, fc):
    p = pl.program_id(2)

    @pl.when(p < nf)
    def _():
        xe = xe_ref[0]
        g = jnp.dot(xe, wg_ref[0], preferred_element_type=F32)
        u = jnp.dot(xe, wu_ref[0], preferred_element_type=F32)
        hid_ref[p] = (g * _sigmoid(g) * u).astype(BF16)

    @pl.when(p >= nf)
    def _():
        y = jnp.dot(hid_ref[0], wd_ref[0, 0:fc, :], preferred_element_type=F32)
        for f in range(1, nf):
            y = y + jnp.dot(hid_ref[f], wd_ref[0, f * fc:(f + 1) * fc, :],
                            preferred_element_type=F32)
        y_ref[0] = (y * gs_ref[0][:, 0:1]).astype(BF16)


def _ffn(xe, gs, wg, wu, wd, cap, rt, fc, oc):
    e, _, d = xe.shape
    dff = wg.shape[2]
    nf = dff // fc
    no = d // oc
    out_col = lambda p: jnp.maximum(p - nf, 0)
    return pl.pallas_call(
        functools.partial(_ffn_kernel, nf=nf, fc=fc),
        grid=(e, cap // rt, nf + no),
        in_specs=[
            pl.BlockSpec((1, rt, d), lambda ex, r, p: (ex, r, 0)),
            pl.BlockSpec((1, rt, 128), lambda ex, r, p: (ex, r, 0)),
            pl.BlockSpec((1, d, fc), lambda ex, r, p: (ex, 0, jnp.minimum(p, nf - 1))),
            pl.BlockSpec((1, d, fc), lambda ex, r, p: (ex, 0, jnp.minimum(p, nf - 1))),
            pl.BlockSpec((1, dff, oc), lambda ex, r, p: (ex, 0, out_col(p))),
        ],
        out_specs=pl.BlockSpec((1, rt, oc), lambda ex, r, p: (ex, r, out_col(p))),
        out_shape=jax.ShapeDtypeStruct((e, cap, d), BF16),
        scratch_shapes=[pltpu.VMEM((nf, rt, fc), BF16)],
        compiler_params=pltpu.CompilerParams(
            dimension_semantics=("parallel", "parallel", "arbitrary"),
            vmem_limit_bytes=V7X_VMEM_LIMIT),
        name="ffn",
    )(xe, gs, wg, wu, wd)


def _combine_kernel(base_ref, npass_ref, x1_ref, rel_ref, gf_ref, y_ref, out_ref,
                    ystage_ref, sem, *, e, cap, nc):
    m = SLOT_WINDOW
    i = pl.program_id(0)
    buf = i % 2
    jota = lax.broadcasted_iota(I32, (e, m, ROUTE_TILE), 1)
    eota = lax.broadcasted_iota(I32, (e, 1, 1), 0)

    def windows(tile, q, into):
        shift = jnp.zeros((e, 1, 1), I32)
        cps = []
        for ex in range(e):
            wp = _window_start(base_ref, tile, ex, e) + q * m
            rp = jnp.minimum(wp, cap - m)
            shift = jnp.where(eota == ex, wp - rp, shift)
            cps.append(pltpu.make_async_copy(
                y_ref.at[ex, pl.ds(pl.multiple_of(rp, SLOT_ALIGN), m)],
                ystage_ref.at[into, pl.ds(ex * m, m)], sem.at[into]))
        return cps, shift

    def expand(q, shift, acc):
        relq = (rel_ref[0] - q * m)[:, None, :]
        hit = (relq >= 0) & (relq < m) & (relq + shift == jota)
        p = hit.astype(F32).reshape(e * m, ROUTE_TILE).astype(BF16)
        return acc + lax.dot_general(p, ystage_ref[buf], (((0,), (0,)), ((), ())),
                                     preferred_element_type=F32)

    @pl.when(i == 0)
    def _():
        for cp in windows(0, 0, 0)[0]:
            cp.start()

    @pl.when(i + 1 < nc)
    def _():
        for cp in windows(i + 1, 0, 1 - buf)[0]:
            cp.start()

    cps, shift = windows(i, 0, buf)
    for cp in cps:
        cp.wait()
    acc = expand(0, shift, x1_ref[...])

    def extra_pass(q, acc):
        cps, shift = windows(i, q, buf)
        for cp in cps:
            cp.start()
        for cp in cps:
            cp.wait()
        return expand(q, shift, acc)

    x2 = lax.fori_loop(1, npass_ref[i], extra_pass, acc)
    out_ref[...] = _rms(x2) * gf_ref[...]


def _combine(base, npass, x1, rel3, gf, y, cap):
    nc, e, _ = rel3.shape
    n, d = x1.shape
    m = SLOT_WINDOW
    grid_spec = pltpu.PrefetchScalarGridSpec(
        num_scalar_prefetch=2,
        grid=(nc,),
        in_specs=[
            pl.BlockSpec((ROUTE_TILE, d), lambda i, *_: (i, 0)),
            pl.BlockSpec((1, e, ROUTE_TILE), lambda i, *_: (i, 0, 0)),
            pl.BlockSpec((1, d), lambda i, *_: (0, 0)),
            pl.BlockSpec(memory_space=pl.ANY),
        ],
        out_specs=pl.BlockSpec((ROUTE_TILE, d), lambda i, *_: (i, 0)),
        scratch_shapes=[
            pltpu.VMEM((2, e * m, d), BF16),
            pltpu.SemaphoreType.DMA((2,)),
        ],
    )
    return pl.pallas_call(
        functools.partial(_combine_kernel, e=e, cap=cap, nc=nc),
        grid_spec=grid_spec,
        out_shape=jax.ShapeDtypeStruct((n, d), F32),
        compiler_params=pltpu.CompilerParams(
            dimension_semantics=("arbitrary",), vmem_limit_bytes=V7X_VMEM_LIMIT),
        name="combine",
    )(base, npass, x1, rel3, gf, y)


def _largest_tile(total, limit, multiple):
    t = min(total, limit)
    while total % t or t % multiple:
        t -= 1
    return t


def _trunk(x, w, cap_factor):
    b, s, d = x.shape
    n = b * s
    e = w["wrt"].shape[0]
    dc = w["cw"].shape[1]
    ds = w["sw"].shape[1]
    cap = cap_factor * n // e
    assert n % ROUTE_TILE == 0 and s % ROUTE_TILE == 0 and cap >= SLOT_WINDOW
    assert cap % SLOT_ALIGN == 0

    tm = _largest_tile(n, 512, 8)
    ts = _largest_tile(s, 256, ROUTE_TILE)
    u = _inproj(x.reshape(n, d), w["g1"], w["w_in"], dc, ds, tm)
    x1, h2, aff3 = _mixer(u.reshape(b, s, dc + 2 * ds), x, w["cw"], w["cb"], w["lg"], w["lb"],
                          w["sw"], w["sb"], w["goc"], w["gos"], w["w_out"], w["g2"], w["wrt"], ts)
    rel3, gate3, base3 = _route(aff3, cap)

    base = jnp.concatenate([base3[:, :, 0], jnp.full((1, e), cap, I32)], axis=0)
    rows = base[1:] - _align_down(base[:-1])
    npass = jnp.maximum((jnp.max(rows, axis=1) + SLOT_WINDOW - 1) // SLOT_WINDOW, 1).astype(I32)
    base = base.reshape(-1)

    xe, gs = _dispatch(base, npass, h2, rel3, gate3, cap)
    rt = _largest_tile(cap, 1024, SLOT_ALIGN)
    fc = _largest_tile(w["wg"].shape[2], 512, 128)
    oc = _largest_tile(d, 1024, 128)
    y = _ffn(xe, gs, w["wg"], w["wu"], w["wd"], cap, rt, fc, oc)
    out = _combine(base, npass, x1.reshape(n, d), rel3, w["gf"], y, cap)
    return out.reshape(b, s, d)


def kernel(x_prompt, x_sample, g_norm1, w_in, conf_dw_w, conf_dw_b, conf_ln_g, conf_ln_b,
           sc_dw_w, sc_dw_b, g_out_conf, g_out_sc, w_out, g_norm2, w_router, w_gate, w_up,
           w_down, g_final):
    depth = w_in.shape[0]
    e = w_router.shape[2]
    cap_factor = 2
    row = lambda v: v.reshape(1, -1)
    layers = []
    for l in range(depth):
        layers.append(dict(
            g1=row(g_norm1[l]), w_in=w_in[l].astype(BF16),
            cw=conf_dw_w[l], cb=row(conf_dw_b[l]), lg=row(conf_ln_g[l]), lb=row(conf_ln_b[l]),
            sw=sc_dw_w[l], sb=row(sc_dw_b[l]), goc=row(g_out_conf[l]), gos=row(g_out_sc[l]),
            w_out=w_out[l].astype(BF16), g2=row(g_norm2[l]),
            wrt=w_router[l].T.astype(BF16),
            wg=w_gate[l].astype(BF16), wu=w_up[l].astype(BF16), wd=w_down[l].astype(BF16),
        ))
    assert depth == 1, "the final norm is fused into the last layer's combine"
    outs = []
    for x in (x_prompt, x_sample):
        w = dict(layers[0], gf=row(g_final))
        outs.append(_trunk(x, w, cap_factor))
    return tuple(outs)
```

```python
import functools

import jax
import jax.numpy as jnp
from jax import lax
from jax.experimental import pallas as pl
from jax.experimental.pallas import tpu as pltpu

F32 = jnp.float32
BF16 = jnp.bfloat16
I32 = jnp.int32

EPS = 1e-6
CONF_HALO = 16
SC_HALO = 8
SUBLANES = 8
CONV_ROWS = 32
ROUTE_TILE = 256
SLOT_WINDOW = 64
SLOT_ALIGN = 16
UNSELECTED = -(1 << 20)
V7X_VMEM_LIMIT = 56 * 1024 * 1024


def _sigmoid(x):
    return 1.0 / (1.0 + jnp.exp(-x))


def _rms(x):
    return x * lax.rsqrt(jnp.mean(x * x, axis=-1, keepdims=True) + EPS)


def _rider_specs(riders, steps, step_of):
    in_specs, out_specs, out_shapes = [], [], []
    for w in riders:
        rows, cols = w.shape
        assert rows % (steps * SLOT_ALIGN) == 0, (rows, steps)
        spec = pl.BlockSpec((rows // steps, cols), lambda *g: (step_of(*g), 0))
        in_specs.append(spec)
        out_specs.append(spec)
        out_shapes.append(jax.ShapeDtypeStruct(w.shape, BF16))
    return in_specs, out_specs, out_shapes


def _cast_riders(srcs, dsts):
    for src, dst in zip(srcs, dsts):
        dst[...] = src[...].astype(BF16)


def _inproj_kernel(*refs, dc, ds, nride):
    x_ref, g_ref, w_ref = refs[:3]
    u_ref = refs[3 + nride]
    _cast_riders(refs[3:3 + nride], refs[4 + nride:])
    h = (_rms(x_ref[...]) * g_ref[...]).astype(BF16)

    def proj(c0, width):
        return jnp.dot(h, w_ref[:, c0:c0 + width], preferred_element_type=F32)

    u_ref[:, 0:dc] = proj(0, dc) * _sigmoid(proj(dc, dc))
    cg = proj(2 * dc + ds, ds)
    u_ref[:, dc:dc + ds] = cg * proj(2 * dc + 2 * ds, ds)
    u_ref[:, dc + ds:dc + 2 * ds] = proj(2 * dc, ds)


def _inproj(x2, g1, w_in, dc, ds, tm, riders=()):
    n, d = x2.shape
    cols = w_in.shape[1]
    r_in, r_out, r_shapes = _rider_specs(riders, n // tm, lambda i: i)
    outs = pl.pallas_call(
        functools.partial(_inproj_kernel, dc=dc, ds=ds, nride=len(riders)),
        grid=(n // tm,),
        in_specs=[
            pl.BlockSpec((tm, d), lambda i: (i, 0)),
            pl.BlockSpec((1, d), lambda i: (0, 0)),
            pl.BlockSpec((d, cols), lambda i: (0, 0), pipeline_mode=pl.Buffered(1)),
        ] + r_in,
        out_specs=[pl.BlockSpec((tm, dc + 2 * ds), lambda i: (i, 0))] + r_out,
        out_shape=[jax.ShapeDtypeStruct((n, dc + 2 * ds), F32)] + r_shapes,
        compiler_params=pltpu.CompilerParams(
            dimension_semantics=("parallel",), vmem_limit_bytes=V7X_VMEM_LIMIT),
        name="inproj",
    )(x2, g1, w_in, *riders)
    return outs[0], outs[1:]


def _sc_shifts(ks):
    pad = (ks - 1) // 2
    return sorted({(SC_HALO + k - pad) % SUBLANES for k in range(ks)} - {0})


MIXER_INPUTS = 17


def _mixer_kernel(*refs, ts, dc, ds, nt, kc, ks, nride):
    (u_ref, ap_ref, an_ref, cp_ref, cn_ref, x_ref,
     cw_ref, cb_ref, lg_ref, lb_ref, sw_ref, sb_ref, goc_ref, gos_ref,
     wout_ref, g2_ref, wrt_ref) = refs[:MIXER_INPUTS]
    outs = refs[MIXER_INPUTS + nride:]
    x1_ref, h2_ref, aff_ref = outs[:3]
    aext_ref, cext_ref, ash_ref, csh_ref, ya_ref, ys_ref = outs[3 + nride:]
    _cast_riders(refs[MIXER_INPUTS:MIXER_INPUTS + nride], outs[3:3 + nride])
    t = pl.program_id(1)
    keep_prev = (t > 0).astype(F32)
    keep_next = (t < nt - 1).astype(F32)
    aext_ref[0:CONF_HALO, :] = ap_ref[0] * keep_prev
    aext_ref[CONF_HALO:CONF_HALO + ts, :] = u_ref[0, :, 0:dc]
    aext_ref[CONF_HALO + ts:2 * CONF_HALO + ts, :] = an_ref[0] * keep_next
    cext_ref[0:SC_HALO, :] = cp_ref[0] * keep_prev
    cext_ref[SC_HALO:SC_HALO + ts, :] = u_ref[0, :, dc:dc + ds]
    cext_ref[SC_HALO + ts:2 * SC_HALO + ts, :] = cn_ref[0] * keep_next

    pad_c = (kc - 1) // 2
    pad_s = (ks - 1) // 2
    for r in range(1, SUBLANES):
        ash_ref[r - 1] = aext_ref[r:r + ash_ref.shape[1], :]
    sc_shifts = _sc_shifts(ks)
    for slot, r in enumerate(sc_shifts):
        csh_ref[slot] = cext_ref[r:r + csh_ref.shape[1], :]

    groups = CONV_ROWS // SUBLANES

    def grouped(v):
        return v.reshape(groups, SUBLANES, v.shape[-1])

    def shifted(ext_ref, sh_ref, o, index):
        q, r = divmod(o, SUBLANES)
        rows = pl.ds(q * SUBLANES, CONV_ROWS)
        return grouped(ext_ref[rows, :] if r == 0 else sh_ref[index(r), rows, :])

    for c in range(ts // CONV_ROWS):
        r0 = c * CONV_ROWS
        conv = jnp.broadcast_to(sb_ref[...][None], (groups, SUBLANES, ds))
        for k in range(ks):
            tap = shifted(cext_ref, csh_ref, r0 + SC_HALO + k - pad_s, sc_shifts.index)
            conv = conv + sw_ref[k][None] * tap
        s = grouped(u_ref[0, r0:r0 + CONV_ROWS, dc + ds:dc + 2 * ds]) * conv
        ys_ref[r0:r0 + CONV_ROWS, :] = (
            _rms(s) * gos_ref[...][None]).reshape(CONV_ROWS, ds).astype(BF16)
    x1 = x_ref[0] + jnp.dot(ys_ref[...], wout_ref[dc:dc + ds, :], preferred_element_type=F32)

    for c in range(ts // CONV_ROWS):
        r0 = c * CONV_ROWS
        acc = jnp.broadcast_to(cb_ref[...][None], (groups, SUBLANES, dc))
        for k in range(kc):
            tap = shifted(aext_ref, ash_ref, r0 + CONF_HALO + k - pad_c, lambda r: r - 1)
            acc = acc + cw_ref[k][None] * tap
        mu = jnp.mean(acc, axis=-1, keepdims=True)
        cen = acc - mu
        var = jnp.mean(cen * cen, axis=-1, keepdims=True)
        a = cen * lax.rsqrt(var + EPS) * lg_ref[...][None] + lb_ref[...][None]
        a = a * _sigmoid(a)
        ya_ref[r0:r0 + CONV_ROWS, :] = (
            _rms(a) * goc_ref[...][None]).reshape(CONV_ROWS, dc).astype(BF16)
    x1 = x1 + jnp.dot(ya_ref[...], wout_ref[0:dc, :], preferred_element_type=F32)
    x1_ref[0] = x1
    h2 = (_rms(x1) * g2_ref[...]).astype(BF16)
    h2_ref[...] = h2
    logits = lax.dot_general(wrt_ref[...], h2, (((1,), (1,)), ((), ())),
                             preferred_element_type=F32)
    ex = jnp.exp(logits - jnp.max(logits, axis=0, keepdims=True))
    aff = ex / jnp.sum(ex, axis=0, keepdims=True)
    for j in range(ts // ROUTE_TILE):
        aff_ref[j] = aff[:, j * ROUTE_TILE:(j + 1) * ROUTE_TILE]


def _replicate_rows(v):
    return jnp.broadcast_to(v, (SUBLANES, v.shape[1]))


def _replicate_taps(w):
    return jnp.broadcast_to(w[:, None, :], (w.shape[0], SUBLANES, w.shape[1]))


def _mixer(u3, x3, cw, cb, lg, lb, sw, sb, goc, gos, wout, g2, wrt, ts, riders=()):
    b, s, d = x3.shape
    dc = cw.shape[1]
    ds = sw.shape[1]
    kc = cw.shape[0]
    ks = sw.shape[0]
    e = wrt.shape[0]
    nt = s // ts
    n = b * s
    hb_c = ts // CONF_HALO
    hb_s = ts // SC_HALO
    col_cv = dc // ds

    def const(shape):
        return pl.BlockSpec(shape, lambda i, t: (0,) * len(shape))

    in_specs = [
        pl.BlockSpec((1, ts, dc + 2 * ds), lambda i, t: (i, t, 0)),
        pl.BlockSpec((1, CONF_HALO, dc), lambda i, t: (i, jnp.maximum(t * hb_c - 1, 0), 0)),
        pl.BlockSpec((1, CONF_HALO, dc),
                     lambda i, t: (i, jnp.minimum((t + 1) * hb_c, s // CONF_HALO - 1), 0)),
        pl.BlockSpec((1, SC_HALO, ds),
                     lambda i, t: (i, jnp.maximum(t * hb_s - 1, 0), col_cv)),
        pl.BlockSpec((1, SC_HALO, ds),
                     lambda i, t: (i, jnp.minimum((t + 1) * hb_s, s // SC_HALO - 1), col_cv)),
        pl.BlockSpec((1, ts, d), lambda i, t: (i, t, 0)),
        const((kc, SUBLANES, dc)), const((SUBLANES, dc)), const((SUBLANES, dc)),
        const((SUBLANES, dc)),
        const((ks, SUBLANES, ds)), const((SUBLANES, ds)), const((SUBLANES, dc)),
        const((SUBLANES, ds)),
        pl.BlockSpec((dc + ds, d), lambda i, t: (0, 0), pipeline_mode=pl.Buffered(1)),
        const((1, d)), const((e, d)),
    ]
    out_specs = [
        pl.BlockSpec((1, ts, d), lambda i, t: (i, t, 0)),
        pl.BlockSpec((ts, d), lambda i, t: (i * nt + t, 0)),
        pl.BlockSpec((ts // ROUTE_TILE, e, ROUTE_TILE), lambda i, t: (i * nt + t, 0, 0)),
    ]
    out_shape = [
        jax.ShapeDtypeStruct((b, s, d), F32),
        jax.ShapeDtypeStruct((n, d), BF16),
        jax.ShapeDtypeStruct((n // ROUTE_TILE, e, ROUTE_TILE), F32),
    ]
    assert len(in_specs) == MIXER_INPUTS
    r_in, r_out, r_shapes = _rider_specs(riders, b * nt, lambda i, t: i * nt + t)
    outs = pl.pallas_call(
        functools.partial(_mixer_kernel, ts=ts, dc=dc, ds=ds, nt=nt, kc=kc, ks=ks,
                          nride=len(riders)),
        grid=(b, nt),
        in_specs=in_specs + r_in,
        out_specs=out_specs + r_out,
        out_shape=out_shape + r_shapes,
        scratch_shapes=[
            pltpu.VMEM((ts + 2 * CONF_HALO, dc), F32),
            pltpu.VMEM((ts + 2 * SC_HALO, ds), F32),
            pltpu.VMEM((SUBLANES - 1, ts + 2 * CONF_HALO - SUBLANES, dc), F32),
            pltpu.VMEM((len(_sc_shifts(ks)), ts + 2 * SC_HALO - SUBLANES, ds), F32),
            pltpu.VMEM((ts, dc), BF16),
            pltpu.VMEM((ts, ds), BF16),
        ],
        compiler_params=pltpu.CompilerParams(
            dimension_semantics=("parallel", "parallel"), vmem_limit_bytes=V7X_VMEM_LIMIT),
        name="mixer",
    )(u3, u3, u3, u3, u3, x3, _replicate_taps(cw), _replicate_rows(cb), _replicate_rows(lg),
      _replicate_rows(lb), _replicate_taps(sw), _replicate_rows(sb), _replicate_rows(goc),
      _replicate_rows(gos), wout, g2, wrt, *riders)
    return outs[:3], outs[3:]


def _route_kernel(aff_ref, rel_ref, gate_ref, base_ref, *, cap, nc, e):
    capf = jnp.float32(cap)

    def count(pred):
        bits = pltpu.bitcast(aff_ref[...], I32)
        per_lane = jnp.sum(pred(bits).astype(F32), axis=0)
        return jnp.sum(per_lane, axis=-1, keepdims=True)

    def search(i, cur):
        cand = cur | lax.shift_left(jnp.int32(1), 30 - i)
        return jnp.where(count(lambda bits: bits >= cand[None]) >= capf, cand, cur)

    thr = lax.fori_loop(0, 31, search, jnp.zeros((e, 1), I32))
    need_eq = capf - count(lambda bits: bits > thr[None])

    row = lax.broadcasted_iota(I32, (ROUTE_TILE, ROUTE_TILE), 0)
    col = lax.broadcasted_iota(I32, (ROUTE_TILE, ROUTE_TILE), 1)
    tri = (row <= col).astype(BF16)

    def chunk(c, carry):
        eq_before, sel_before = carry
        a = aff_ref[c]
        b = pltpu.bitcast(a, I32)
        gt = b > thr
        eq = (b == thr)
        eqf = eq.astype(F32)
        eq_incl = jnp.dot(eqf.astype(BF16), tri, preferred_element_type=F32)
        eq_rank = eq_incl - eqf + eq_before
        sel = gt | (eq & (eq_rank < need_eq))
        self_ = sel.astype(F32)
        sel_incl = jnp.dot(self_.astype(BF16), tri, preferred_element_type=F32)
        slot = sel_incl - self_ + sel_before
        window = jnp.floor(sel_before * (1.0 / SLOT_ALIGN)) * SLOT_ALIGN
        rel_ref[c] = jnp.where(sel, (slot - window).astype(I32), UNSELECTED)
        gate_ref[c] = jnp.where(sel, a, 0.0)
        base_ref[c] = jnp.broadcast_to(sel_before, (e, 128)).astype(I32)
        return (eq_before + eq_incl[:, ROUTE_TILE - 1:ROUTE_TILE],
                sel_before + sel_incl[:, ROUTE_TILE - 1:ROUTE_TILE])

    zero = jnp.zeros((e, 1), F32)
    lax.fori_loop(0, nc, chunk, (zero, zero))


def _route(aff3, cap):
    nc, e, _ = aff3.shape
    full = lambda shape: pl.BlockSpec(shape, lambda: (0,) * len(shape))
    return pl.pallas_call(
        functools.partial(_route_kernel, cap=cap, nc=nc, e=e),
        in_specs=[full((nc, e, ROUTE_TILE))],
        out_specs=[full((nc, e, ROUTE_TILE)), full((nc, e, ROUTE_TILE)), full((nc, e, 128))],
        out_shape=[
            jax.ShapeDtypeStruct((nc, e, ROUTE_TILE), I32),
            jax.ShapeDtypeStruct((nc, e, ROUTE_TILE), F32),
            jax.ShapeDtypeStruct((nc, e, 128), I32),
        ],
        name="route",
    )(aff3)


def _align_down(slot):
    shift = SLOT_ALIGN.bit_length() - 1
    return (slot >> shift) << shift


def _window_start(base_ref, i, ex, e):
    return _align_down(base_ref[i * e + ex])


def _dispatch_kernel(base_ref, npass_ref, h2_ref, rel_ref, gate_ref,
                     xe_ref, gs_ref,
                     stage_ref, gstage_ref, carry_ref, gcarry_ref, count_ref, sem,
                     *, e, cap, nc):
    m = SLOT_WINDOW
    i = pl.program_id(0)

    @pl.when(i == 0)
    def _():
        carry_ref[...] = jnp.zeros_like(carry_ref)
        gcarry_ref[...] = jnp.zeros_like(gcarry_ref)
        count_ref[0] = 0

    def copies(buf, ex, dst_row):
        rows = pl.ds(ex * m, m)
        return (
            pltpu.make_async_copy(stage_ref.at[buf, rows], xe_ref.at[ex, pl.ds(dst_row, m)],
                                  sem.at[0]),
            pltpu.make_async_copy(gstage_ref.at[buf, rows], gs_ref.at[ex, pl.ds(dst_row, m)],
                                  sem.at[1]),
        )

    def wait_all(buf):
        for ex in range(e):
            for cp in copies(buf, ex, 0):
                cp.wait()

    h2 = h2_ref[...]
    jota = lax.broadcasted_iota(I32, (e, m, ROUTE_TILE), 1)

    def one_pass(q, _):
        done = count_ref[0]
        buf = done % 2
        relq = rel_ref[0] - q * m
        hit = relq[:, None, :] == jota
        p = hit.astype(F32)
        stage_ref[buf] = jnp.dot(p.reshape(e * m, ROUTE_TILE).astype(BF16), h2,
                                 preferred_element_type=F32).astype(BF16)
        gsel = jnp.sum(p * gate_ref[0][:, None, :], axis=-1, keepdims=True)
        gstage_ref[buf] = jnp.broadcast_to(gsel, (e, m, 128)).reshape(e * m, 128)

        for ex in range(e):
            head = pl.ds(ex * m, SLOT_ALIGN)
            wp = _window_start(base_ref, i, ex, e)
            hi = base_ref[(i + 1) * e + ex]

            @pl.when(q == 0)
            def _():
                stage_ref[buf, head, :] = stage_ref[buf, head, :] + carry_ref[ex]
                gstage_ref[buf, head, :] = gstage_ref[buf, head, :] + gcarry_ref[ex]
                carry_ref[ex] = jnp.zeros((SLOT_ALIGN, carry_ref.shape[-1]), BF16)
                gcarry_ref[ex] = jnp.zeros((SLOT_ALIGN, 128), F32)

            off = _align_down(hi) - wp - q * m

            @pl.when((off >= 0) & (off < m))
            def _():
                tail = pl.ds(pl.multiple_of(ex * m + off, SLOT_ALIGN), SLOT_ALIGN)
                carry_ref[ex] = stage_ref[buf, tail, :]
                gcarry_ref[ex] = gstage_ref[buf, tail, :]

        @pl.when(done > 0)
        def _():
            wait_all(1 - buf)

        for ex in range(e):
            wp = _window_start(base_ref, i, ex, e)
            hi = base_ref[(i + 1) * e + ex]
            needed = (hi - wp > q * m) | (q == 0)
            dst = jnp.where(needed, wp + q * m, cap)
            for cp in copies(buf, ex, pl.multiple_of(dst, SLOT_ALIGN)):
                cp.start()
        count_ref[0] = done + 1
        return 0

    lax.fori_loop(0, npass_ref[i], one_pass, 0)

    @pl.when(i == nc - 1)
    def _():
        wait_all((count_ref[0] - 1) % 2)
        stage_ref[0] = jnp.zeros(stage_ref.shape[1:], BF16)
        gstage_ref[0] = jnp.zeros(gstage_ref.shape[1:], F32)
        for ex in range(e):
            for cp in copies(0, ex, cap):
                cp.start()
        wait_all(0)


def _dispatch(base, npass, h2, rel3, gate3, cap):
    nc, e, _ = rel3.shape
    d = h2.shape[1]
    m = SLOT_WINDOW
    grid_spec = pltpu.PrefetchScalarGridSpec(
        num_scalar_prefetch=2,
        grid=(nc,),
        in_specs=[
            pl.BlockSpec((ROUTE_TILE, d), lambda i, *_: (i, 0)),
            pl.BlockSpec((1, e, ROUTE_TILE), lambda i, *_: (i, 0, 0)),
            pl.BlockSpec((1, e, ROUTE_TILE), lambda i, *_: (i, 0, 0)),
        ],
        out_specs=[pl.BlockSpec(memory_space=pl.ANY), pl.BlockSpec(memory_space=pl.ANY)],
        scratch_shapes=[
            pltpu.VMEM((2, e * m, d), BF16),
            pltpu.VMEM((2, e * m, 128), F32),
            pltpu.VMEM((e, SLOT_ALIGN, d), BF16),
            pltpu.VMEM((e, SLOT_ALIGN, 128), F32),
            pltpu.SMEM((1,), I32),
            pltpu.SemaphoreType.DMA((2,)),
        ],
    )
    return pl.pallas_call(
        functools.partial(_dispatch_kernel, e=e, cap=cap, nc=nc),
        grid_spec=grid_spec,
        out_shape=[
            jax.ShapeDtypeStruct((e, cap + m, d), BF16),
            jax.ShapeDtypeStruct((e, cap + m, 128), F32),
        ],
        compiler_params=pltpu.CompilerParams(
            dimension_semantics=("arbitrary",), vmem_limit_bytes=V7X_VMEM_LIMIT),
        name="dispatch",
    )(base, npass, h2, rel3, gate3)


def _ffn_kernel(xe_ref, gs_ref, wg_ref, wu_ref, wd_ref, y_ref, hid_ref, *, nf, fc):
    p = pl.program_id(2)

    @pl.when(p < nf)
    def _():
        xe = xe_ref[0]
        g = jnp.dot(xe, wg_ref[0], preferred_element_type=F32)
        u = jnp.dot(xe, wu_ref[0], preferred_element_type=F32)
        hid_ref[p] = (g * _sigmoid(g) * u).astype(BF16)

    @pl.when(p >= nf)
    def _():
        y = jnp.dot(hid_ref[0], wd_ref[0, 0:fc, :], preferred_element_type=F32)
        for f in range(1, nf):
            y = y + jnp.dot(hid_ref[f], wd_ref[0, f * fc:(f + 1) * fc, :],
                            preferred_element_type=F32)
        y_ref[0] = (y * gs_ref[0][:, 0:1]).astype(BF16)


def _ffn(xe, gs, wg, wu, wd, cap, rt, fc, oc):
    e, _, d = xe.shape
    dff = wg.shape[2]
    nf = dff // fc
    no = d // oc
    out_col = lambda p: jnp.maximum(p - nf, 0)
    return pl.pallas_call(
        functools.partial(_ffn_kernel, nf=nf, fc=fc),
        grid=(e, cap // rt, nf + no),
        in_specs=[
            pl.BlockSpec((1, rt, d), lambda ex, r, p: (ex, r, 0)),
            pl.BlockSpec((1, rt, 128), lambda ex, r, p: (ex, r, 0)),
            pl.BlockSpec((1, d, fc), lambda ex, r, p: (ex, 0, jnp.minimum(p, nf - 1))),
            pl.BlockSpec((1, d, fc), lambda ex, r, p: (ex, 0, jnp.minimum(p, nf - 1))),
            pl.BlockSpec((1, dff, oc), lambda ex, r, p: (ex, 0, out_col(p))),
        ],
        out_specs=pl.BlockSpec((1, rt, oc), lambda ex, r, p: (ex, r, out_col(p))),
        out_shape=jax.ShapeDtypeStruct((e, cap, d), BF16),
        scratch_shapes=[pltpu.VMEM((nf, rt, fc), BF16)],
        compiler_params=pltpu.CompilerParams(
            dimension_semantics=("parallel", "parallel", "arbitrary"),
            vmem_limit_bytes=V7X_VMEM_LIMIT),
        name="ffn",
    )(xe, gs, wg, wu, wd)


def _combine_kernel(base_ref, npass_ref, x1_ref, rel_ref, gf_ref, y_ref, out_ref,
                    ystage_ref, sem, *, e, cap, nc):
    m = SLOT_WINDOW
    i = pl.program_id(0)
    buf = i % 2
    jota = lax.broadcasted_iota(I32, (e, m, ROUTE_TILE), 1)
    eota = lax.broadcasted_iota(I32, (e, 1, 1), 0)

    def windows(tile, q, into):
        shift = jnp.zeros((e, 1, 1), I32)
        cps = []
        for ex in range(e):
            wp = _window_start(base_ref, tile, ex, e) + q * m
            rp = jnp.minimum(wp, cap - m)
            shift = jnp.where(eota == ex, wp - rp, shift)
            cps.append(pltpu.make_async_copy(
                y_ref.at[ex, pl.ds(pl.multiple_of(rp, SLOT_ALIGN), m)],
                ystage_ref.at[into, pl.ds(ex * m, m)], sem.at[into]))
        return cps, shift

    def expand(q, shift, acc):
        relq = (rel_ref[0] - q * m)[:, None, :]
        hit = (relq >= 0) & (relq < m) & (relq + shift == jota)
        p = hit.astype(F32).reshape(e * m, ROUTE_TILE).astype(BF16)
        return acc + lax.dot_general(p, ystage_ref[buf], (((0,), (0,)), ((), ())),
                                     preferred_element_type=F32)

    @pl.when(i == 0)
    def _():
        for cp in windows(0, 0, 0)[0]:
            cp.start()

    @pl.when(i + 1 < nc)
    def _():
        for cp in windows(i + 1, 0, 1 - buf)[0]:
            cp.start()

    cps, shift = windows(i, 0, buf)
    for cp in cps:
        cp.wait()
    acc = expand(0, shift, x1_ref[...])

    def extra_pass(q, acc):
        cps, shift = windows(i, q, buf)
        for cp in cps:
            cp.start()
        for cp in cps:
            cp.wait()
        return expand(q, shift, acc)

    x2 = lax.fori_loop(1, npass_ref[i], extra_pass, acc)
    out_ref[...] = _rms(x2) * gf_ref[...]


def _combine(base, npass, x1, rel3, gf, y, cap):
    nc, e, _ = rel3.shape
    n, d = x1.shape
    m = SLOT_WINDOW
    grid_spec = pltpu.PrefetchScalarGridSpec(
        num_scalar_prefetch=2,
        grid=(nc,),
        in_specs=[
            pl.BlockSpec((ROUTE_TILE, d), lambda i, *_: (i, 0)),
            pl.BlockSpec((1, e, ROUTE_TILE), lambda i, *_: (i, 0, 0)),
            pl.BlockSpec((1, d), lambda i, *_: (0, 0)),
            pl.BlockSpec(memory_space=pl.ANY),
        ],
        out_specs=pl.BlockSpec((ROUTE_TILE, d), lambda i, *_: (i, 0)),
        scratch_shapes=[
            pltpu.VMEM((2, e * m, d), BF16),
            pltpu.SemaphoreType.DMA((2,)),
        ],
    )
    return pl.pallas_call(
        functools.partial(_combine_kernel, e=e, cap=cap, nc=nc),
        grid_spec=grid_spec,
        out_shape=jax.ShapeDtypeStruct((n, d), F32),
        compiler_params=pltpu.CompilerParams(
            dimension_semantics=("arbitrary",), vmem_limit_bytes=V7X_VMEM_LIMIT),
        name="combine",
    )(base, npass, x1, rel3, gf, y)


def _largest_tile(total, limit, multiple):
    t = min(total, limit)
    while total % t or t % multiple:
        t -= 1
    return t


def _trunk(x, w, cap_factor, experts):
    b, s, d = x.shape
    n = b * s
    e = w["wrt"].shape[0]
    dc = w["cw"].shape[1]
    ds = w["sw"].shape[1]
    cap = cap_factor * n // e
    assert n % ROUTE_TILE == 0 and s % ROUTE_TILE == 0 and cap >= SLOT_WINDOW
    assert cap % SLOT_ALIGN == 0

    tm = _largest_tile(n, 256, 8)
    ts = _largest_tile(s, 256, ROUTE_TILE)
    flat = lambda v: v.reshape(-1, v.shape[-1])
    ride_in = [] if experts else [flat(w["wg32"]), flat(w["wu32"])]
    ride_mix = [] if experts else [flat(w["wd32"])]
    u, cast_in = _inproj(x.reshape(n, d), w["g1"], w["w_in"], dc, ds, tm, ride_in)
    (x1, h2, aff3), cast_mix = _mixer(
        u.reshape(b, s, dc + 2 * ds), x, w["cw"], w["cb"], w["lg"], w["lb"], w["sw"], w["sb"],
        w["goc"], w["gos"], w["w_out"], w["g2"], w["wrt"], ts, ride_mix)
    if not experts:
        experts = tuple(v.reshape(w["wg32"].shape[0], -1, v.shape[-1])
                        for v in (*cast_in, *cast_mix))
    wg, wu, wd = experts
    rel3, gate3, base3 = _route(aff3, cap)

    base = jnp.concatenate([base3[:, :, 0], jnp.full((1, e), cap, I32)], axis=0)
    rows = base[1:] - _align_down(base[:-1])
    npass = jnp.maximum((jnp.max(rows, axis=1) + SLOT_WINDOW - 1) // SLOT_WINDOW, 1).astype(I32)
    base = base.reshape(-1)

    xe, gs = _dispatch(base, npass, h2, rel3, gate3, cap)
    rt = _largest_tile(cap, 1024, SLOT_ALIGN)
    fc = _largest_tile(wg.shape[2], 512, 128)
    oc = _largest_tile(d, 1024, 128)
    y = _ffn(xe, gs, wg, wu, wd, cap, rt, fc, oc)
    out = _combine(base, npass, x1.reshape(n, d), rel3, w["gf"], y, cap)
    return out.reshape(b, s, d), experts


def kernel(x_prompt, x_sample, g_norm1, w_in, conf_dw_w, conf_dw_b, conf_ln_g, conf_ln_b,
           sc_dw_w, sc_dw_b, g_out_conf, g_out_sc, w_out, g_norm2, w_router, w_gate, w_up,
           w_down, g_final):
    depth = w_in.shape[0]
    e = w_router.shape[2]
    cap_factor = 2
    row = lambda v: v.reshape(1, -1)
    layers = []
    for l in range(depth):
        layers.append(dict(
            g1=row(g_norm1[l]), w_in=w_in[l].astype(BF16),
            cw=conf_dw_w[l], cb=row(conf_dw_b[l]), lg=row(conf_ln_g[l]), lb=row(conf_ln_b[l]),
            sw=sc_dw_w[l], sb=row(sc_dw_b[l]), goc=row(g_out_conf[l]), gos=row(g_out_sc[l]),
            w_out=w_out[l].astype(BF16), g2=row(g_norm2[l]),
            wrt=w_router[l].T.astype(BF16),
            wg32=w_gate[l], wu32=w_up[l], wd32=w_down[l],
        ))
    assert depth == 1, "the final norm is fused into the last layer's combine"
    outs = []
    experts = None
    for x in (x_prompt, x_sample):
        w = dict(layers[0], gf=row(g_final))
        out, experts = _trunk(x, w, cap_factor, experts)
        outs.append(out)
    return tuple(outs)
```

```python
import functools

import jax
import jax.numpy as jnp
from jax import lax
from jax.experimental import pallas as pl
from jax.experimental.pallas import tpu as pltpu

F32 = jnp.float32
BF16 = jnp.bfloat16
I32 = jnp.int32

EPS = 1e-6
CONF_HALO = 16
SC_HALO = 8
SUBLANES = 8
CONV_ROWS = 32
ROUTE_TILE = 256
SLOT_WINDOW = 64
SLOT_ALIGN = 16
UNSELECTED = -(1 << 20)
V7X_VMEM_LIMIT = 56 * 1024 * 1024


def _sigmoid(x):
    return 1.0 / (1.0 + jnp.exp(-x))


def _rms(x):
    return x * lax.rsqrt(jnp.mean(x * x, axis=-1, keepdims=True) + EPS)


def _rider_specs(riders, steps, step_of):
    in_specs, out_specs, out_shapes = [], [], []
    for w in riders:
        rows, cols = w.shape
        assert rows % (steps * SLOT_ALIGN) == 0, (rows, steps)
        spec = pl.BlockSpec((rows // steps, cols), lambda *g: (step_of(*g), 0))
        in_specs.append(spec)
        out_specs.append(spec)
        out_shapes.append(jax.ShapeDtypeStruct(w.shape, BF16))
    return in_specs, out_specs, out_shapes


def _cast_riders(srcs, dsts):
    for src, dst in zip(srcs, dsts):
        dst[...] = src[...].astype(BF16)


def _inproj_kernel(*refs, dc, ds, nride):
    x_ref, g_ref, w_ref = refs[:3]
    u_ref = refs[3 + nride]
    _cast_riders(refs[3:3 + nride], refs[4 + nride:])
    h = (_rms(x_ref[...]) * g_ref[...]).astype(BF16)

    def proj(c0, width):
        return jnp.dot(h, w_ref[:, c0:c0 + width], preferred_element_type=F32)

    u_ref[:, 0:dc] = proj(0, dc) * _sigmoid(proj(dc, dc))
    cg = proj(2 * dc + ds, ds)
    u_ref[:, dc:dc + ds] = cg * proj(2 * dc + 2 * ds, ds)
    u_ref[:, dc + ds:dc + 2 * ds] = proj(2 * dc, ds)


def _inproj(x2, g1, w_in, dc, ds, tm, riders=()):
    n, d = x2.shape
    cols = w_in.shape[1]
    r_in, r_out, r_shapes = _rider_specs(riders, n // tm, lambda i: i)
    outs = pl.pallas_call(
        functools.partial(_inproj_kernel, dc=dc, ds=ds, nride=len(riders)),
        grid=(n // tm,),
        in_specs=[
            pl.BlockSpec((tm, d), lambda i: (i, 0)),
            pl.BlockSpec((1, d), lambda i: (0, 0)),
            pl.BlockSpec((d, cols), lambda i: (0, 0), pipeline_mode=pl.Buffered(1)),
        ] + r_in,
        out_specs=[pl.BlockSpec((tm, dc + 2 * ds), lambda i: (i, 0))] + r_out,
        out_shape=[jax.ShapeDtypeStruct((n, dc + 2 * ds), F32)] + r_shapes,
        compiler_params=pltpu.CompilerParams(
            dimension_semantics=("parallel",), vmem_limit_bytes=V7X_VMEM_LIMIT),
        name="inproj",
    )(x2, g1, w_in, *riders)
    return outs[0], outs[1:]


def _sc_shifts(ks):
    pad = (ks - 1) // 2
    return sorted({(SC_HALO + k - pad) % SUBLANES for k in range(ks)} - {0})


MIXER_INPUTS = 17


def _mixer_kernel(*refs, ts, dc, ds, nt, kc, ks, nride):
    (u_ref, ap_ref, an_ref, cp_ref, cn_ref, x_ref,
     cw_ref, cb_ref, lg_ref, lb_ref, sw_ref, sb_ref, goc_ref, gos_ref,
     wout_ref, g2_ref, wrt_ref) = refs[:MIXER_INPUTS]
    outs = refs[MIXER_INPUTS + nride:]
    x1_ref, h2_ref, aff_ref = outs[:3]
    aext_ref, cext_ref, ash_ref, csh_ref, ya_ref, ys_ref = outs[3 + nride:]
    _cast_riders(refs[MIXER_INPUTS:MIXER_INPUTS + nride], outs[3:3 + nride])
    t = pl.program_id(1)
    keep_prev = (t > 0).astype(F32)
    keep_next = (t < nt - 1).astype(F32)
    aext_ref[0:CONF_HALO, :] = ap_ref[0] * keep_prev
    aext_ref[CONF_HALO:CONF_HALO + ts, :] = u_ref[0, :, 0:dc]
    aext_ref[CONF_HALO + ts:2 * CONF_HALO + ts, :] = an_ref[0] * keep_next
    cext_ref[0:SC_HALO, :] = cp_ref[0] * keep_prev
    cext_ref[SC_HALO:SC_HALO + ts, :] = u_ref[0, :, dc:dc + ds]
    cext_ref[SC_HALO + ts:2 * SC_HALO + ts, :] = cn_ref[0] * keep_next

    pad_c = (kc - 1) // 2
    pad_s = (ks - 1) // 2
    for r in range(1, SUBLANES):
        ash_ref[r - 1] = aext_ref[r:r + ash_ref.shape[1], :]
    sc_shifts = _sc_shifts(ks)
    for slot, r in enumerate(sc_shifts):
        csh_ref[slot] = cext_ref[r:r + csh_ref.shape[1], :]

    groups = CONV_ROWS // SUBLANES

    def grouped(v):
        return v.reshape(groups, SUBLANES, v.shape[-1])

    def shifted(ext_ref, sh_ref, o, index):
        q, r = divmod(o, SUBLANES)
        rows = pl.ds(q * SUBLANES, CONV_ROWS)
        return grouped(ext_ref[rows, :] if r == 0 else sh_ref[index(r), rows, :])

    for c in range(ts // CONV_ROWS):
        r0 = c * CONV_ROWS
        conv = jnp.broadcast_to(sb_ref[...][None], (groups, SUBLANES, ds))
        for k in range(ks):
            tap = shifted(cext_ref, csh_ref, r0 + SC_HALO + k - pad_s, sc_shifts.index)
            conv = conv + sw_ref[k][None] * tap
        s = grouped(u_ref[0, r0:r0 + CONV_ROWS, dc + ds:dc + 2 * ds]) * conv
        ys_ref[r0:r0 + CONV_ROWS, :] = (
            _rms(s) * gos_ref[...][None]).reshape(CONV_ROWS, ds).astype(BF16)
    x1 = x_ref[0] + jnp.dot(ys_ref[...], wout_ref[dc:dc + ds, :], preferred_element_type=F32)

    for c in range(ts // CONV_ROWS):
        r0 = c * CONV_ROWS
        acc = jnp.broadcast_to(cb_ref[...][None], (groups, SUBLANES, dc))
        for k in range(kc):
            tap = shifted(aext_ref, ash_ref, r0 + CONF_HALO + k - pad_c, lambda r: r - 1)
            acc = acc + cw_ref[k][None] * tap
        mu = jnp.mean(acc, axis=-1, keepdims=True)
        cen = acc - mu
        var = jnp.mean(cen * cen, axis=-1, keepdims=True)
        a = cen * lax.rsqrt(var + EPS) * lg_ref[...][None] + lb_ref[...][None]
        a = a * _sigmoid(a)
        ya_ref[r0:r0 + CONV_ROWS, :] = (
            _rms(a) * goc_ref[...][None]).reshape(CONV_ROWS, dc).astype(BF16)
    x1 = x1 + jnp.dot(ya_ref[...], wout_ref[0:dc, :], preferred_element_type=F32)
    x1_ref[0] = x1
    h2 = (_rms(x1) * g2_ref[...]).astype(BF16)
    h2_ref[...] = h2
    logits = lax.dot_general(wrt_ref[...], h2, (((1,), (1,)), ((), ())),
                             preferred_element_type=F32)
    ex = jnp.exp(logits - jnp.max(logits, axis=0, keepdims=True))
    aff = ex / jnp.sum(ex, axis=0, keepdims=True)
    for j in range(ts // ROUTE_TILE):
        aff_ref[j] = aff[:, j * ROUTE_TILE:(j + 1) * ROUTE_TILE]


def _replicate_rows(v):
    return jnp.broadcast_to(v, (SUBLANES, v.shape[1]))


def _replicate_taps(w):
    return jnp.broadcast_to(w[:, None, :], (w.shape[0], SUBLANES, w.shape[1]))


def _mixer(u3, x3, cw, cb, lg, lb, sw, sb, goc, gos, wout, g2, wrt, ts, riders=()):
    b, s, d = x3.shape
    dc = cw.shape[1]
    ds = sw.shape[1]
    kc = cw.shape[0]
    ks = sw.shape[0]
    e = wrt.shape[0]
    nt = s // ts
    n = b * s
    hb_c = ts // CONF_HALO
    hb_s = ts // SC_HALO
    col_cv = dc // ds

    def const(shape):
        return pl.BlockSpec(shape, lambda i, t: (0,) * len(shape))

    in_specs = [
        pl.BlockSpec((1, ts, dc + 2 * ds), lambda i, t: (i, t, 0)),
        pl.BlockSpec((1, CONF_HALO, dc), lambda i, t: (i, jnp.maximum(t * hb_c - 1, 0), 0)),
        pl.BlockSpec((1, CONF_HALO, dc),
                     lambda i, t: (i, jnp.minimum((t + 1) * hb_c, s // CONF_HALO - 1), 0)),
        pl.BlockSpec((1, SC_HALO, ds),
                     lambda i, t: (i, jnp.maximum(t * hb_s - 1, 0), col_cv)),
        pl.BlockSpec((1, SC_HALO, ds),
                     lambda i, t: (i, jnp.minimum((t + 1) * hb_s, s // SC_HALO - 1), col_cv)),
        pl.BlockSpec((1, ts, d), lambda i, t: (i, t, 0)),
        const((kc, SUBLANES, dc)), const((SUBLANES, dc)), const((SUBLANES, dc)),
        const((SUBLANES, dc)),
        const((ks, SUBLANES, ds)), const((SUBLANES, ds)), const((SUBLANES, dc)),
        const((SUBLANES, ds)),
        pl.BlockSpec((dc + ds, d), lambda i, t: (0, 0), pipeline_mode=pl.Buffered(1)),
        const((1, d)), const((e, d)),
    ]
    out_specs = [
        pl.BlockSpec((1, ts, d), lambda i, t: (i, t, 0)),
        pl.BlockSpec((ts, d), lambda i, t: (i * nt + t, 0)),
        pl.BlockSpec((ts // ROUTE_TILE, e, ROUTE_TILE), lambda i, t: (i * nt + t, 0, 0)),
    ]
    out_shape = [
        jax.ShapeDtypeStruct((b, s, d), F32),
        jax.ShapeDtypeStruct((n, d), BF16),
        jax.ShapeDtypeStruct((n // ROUTE_TILE, e, ROUTE_TILE), F32),
    ]
    assert len(in_specs) == MIXER_INPUTS
    r_in, r_out, r_shapes = _rider_specs(riders, b * nt, lambda i, t: i * nt + t)
    outs = pl.pallas_call(
        functools.partial(_mixer_kernel, ts=ts, dc=dc, ds=ds, nt=nt, kc=kc, ks=ks,
                          nride=len(riders)),
        grid=(b, nt),
        in_specs=in_specs + r_in,
        out_specs=out_specs + r_out,
        out_shape=out_shape + r_shapes,
        scratch_shapes=[
            pltpu.VMEM((ts + 2 * CONF_HALO, dc), F32),
            pltpu.VMEM((ts + 2 * SC_HALO, ds), F32),
            pltpu.VMEM((SUBLANES - 1, ts + 2 * CONF_HALO - SUBLANES, dc), F32),
            pltpu.VMEM((len(_sc_shifts(ks)), ts + 2 * SC_HALO - SUBLANES, ds), F32),
            pltpu.VMEM((ts, dc), BF16),
            pltpu.VMEM((ts, ds), BF16),
        ],
        compiler_params=pltpu.CompilerParams(
            dimension_semantics=("parallel", "parallel"), vmem_limit_bytes=V7X_VMEM_LIMIT),
        name="mixer",
    )(u3, u3, u3, u3, u3, x3, _replicate_taps(cw), _replicate_rows(cb), _replicate_rows(lg),
      _replicate_rows(lb), _replicate_taps(sw), _replicate_rows(sb), _replicate_rows(goc),
      _replicate_rows(gos), wout, g2, wrt, *riders)
    return outs[:3], outs[3:]


def _route_kernel(aff_ref, rel_ref, gate_ref, base_ref, before_ref, *, cap, nc, e):
    capf = jnp.float32(cap)

    def count(pred):
        bits = pltpu.bitcast(aff_ref[...], I32)
        per_lane = jnp.sum(pred(bits).astype(F32), axis=0)
        return jnp.sum(per_lane, axis=-1, keepdims=True)

    def search(i, cur):
        cand = cur | lax.shift_left(jnp.int32(1), 30 - i)
        return jnp.where(count(lambda bits: bits >= cand[None]) >= capf, cand, cur)

    thr = lax.fori_loop(0, 31, search, jnp.zeros((e, 1), I32))
    need_eq = capf - count(lambda bits: bits > thr[None])

    row = lax.broadcasted_iota(I32, (ROUTE_TILE, ROUTE_TILE), 0)
    col = lax.broadcasted_iota(I32, (ROUTE_TILE, ROUTE_TILE), 1)
    tri = (row <= col).astype(BF16)

    def rank(flags):
        within = jnp.dot(flags.reshape(nc * e, ROUTE_TILE).astype(BF16), tri,
                         preferred_element_type=F32).reshape(nc, e, ROUTE_TILE)
        before_ref[...] = jnp.broadcast_to(within[:, :, ROUTE_TILE - 1:], (nc, e, 128))

        def tile_prefix(c, running):
            total = before_ref[c]
            before_ref[c] = running
            return running + total

        lax.fori_loop(0, nc, tile_prefix, jnp.zeros((e, 128), F32))
        before = before_ref[...][:, :, 0:1]
        return within - flags + before, before

    aff = aff_ref[...]
    bits = pltpu.bitcast(aff, I32)
    eq = bits == thr[None]
    eq_rank, _ = rank(eq.astype(F32))
    sel = (bits > thr[None]) | (eq & (eq_rank < need_eq[None]))
    slot, sel_before = rank(sel.astype(F32))
    window = jnp.floor(sel_before * (1.0 / SLOT_ALIGN)) * SLOT_ALIGN
    rel_ref[...] = jnp.where(sel, (slot - window).astype(I32), UNSELECTED)
    gate_ref[...] = jnp.where(sel, aff, 0.0)
    base_ref[...] = before_ref[...].astype(I32)


def _route(aff3, cap):
    nc, e, _ = aff3.shape
    full = lambda shape: pl.BlockSpec(shape, lambda: (0,) * len(shape))
    return pl.pallas_call(
        functools.partial(_route_kernel, cap=cap, nc=nc, e=e),
        in_specs=[full((nc, e, ROUTE_TILE))],
        out_specs=[full((nc, e, ROUTE_TILE)), full((nc, e, ROUTE_TILE)), full((nc, e, 128))],
        out_shape=[
            jax.ShapeDtypeStruct((nc, e, ROUTE_TILE), I32),
            jax.ShapeDtypeStruct((nc, e, ROUTE_TILE), F32),
            jax.ShapeDtypeStruct((nc, e, 128), I32),
        ],
        scratch_shapes=[pltpu.VMEM((nc, e, 128), F32)],
        compiler_params=pltpu.CompilerParams(vmem_limit_bytes=V7X_VMEM_LIMIT),
        name="route",
    )(aff3)


def _align_down(slot):
    shift = SLOT_ALIGN.bit_length() - 1
    return (slot >> shift) << shift


def _window_start(base_ref, i, ex, e):
    return _align_down(base_ref[i * e + ex])


def _dispatch_kernel(base_ref, npass_ref, h2_ref, rel_ref, gate_ref,
                     xe_ref, gs_ref,
                     stage_ref, gstage_ref, carry_ref, gcarry_ref, count_ref, sem,
                     *, e, cap, nc):
    m = SLOT_WINDOW
    i = pl.program_id(0)

    @pl.when(i == 0)
    def _():
        carry_ref[...] = jnp.zeros_like(carry_ref)
        gcarry_ref[...] = jnp.zeros_like(gcarry_ref)
        count_ref[0] = 0

    def copies(buf, ex, dst_row):
        rows = pl.ds(ex * m, m)
        return (
            pltpu.make_async_copy(stage_ref.at[buf, rows], xe_ref.at[ex, pl.ds(dst_row, m)],
                                  sem.at[0]),
            pltpu.make_async_copy(gstage_ref.at[buf, rows], gs_ref.at[ex, pl.ds(dst_row, m)],
                                  sem.at[1]),
        )

    def wait_all(buf):
        for ex in range(e):
            for cp in copies(buf, ex, 0):
                cp.wait()

    h2 = h2_ref[...]
    jota = lax.broadcasted_iota(I32, (e, m, ROUTE_TILE), 1)

    def one_pass(q, _):
        done = count_ref[0]
        buf = done % 2
        relq = rel_ref[0] - q * m
        hit = relq[:, None, :] == jota
        p = hit.astype(F32)
        stage_ref[buf] = jnp.dot(p.reshape(e * m, ROUTE_TILE).astype(BF16), h2,
                                 preferred_element_type=F32).astype(BF16)
        gsel = jnp.sum(p * gate_ref[0][:, None, :], axis=-1, keepdims=True)
        gstage_ref[buf] = jnp.broadcast_to(gsel, (e, m, 128)).reshape(e * m, 128)

        for ex in range(e):
            head = pl.ds(ex * m, SLOT_ALIGN)
            wp = _window_start(base_ref, i, ex, e)
            hi = base_ref[(i + 1) * e + ex]

            @pl.when(q == 0)
            def _():
                stage_ref[buf, head, :] = stage_ref[buf, head, :] + carry_ref[ex]
                gstage_ref[buf, head, :] = gstage_ref[buf, head, :] + gcarry_ref[ex]
                carry_ref[ex] = jnp.zeros((SLOT_ALIGN, carry_ref.shape[-1]), BF16)
                gcarry_ref[ex] = jnp.zeros((SLOT_ALIGN, 128), F32)

            off = _align_down(hi) - wp - q * m

            @pl.when((off >= 0) & (off < m))
            def _():
                tail = pl.ds(pl.multiple_of(ex * m + off, SLOT_ALIGN), SLOT_ALIGN)
                carry_ref[ex] = stage_ref[buf, tail, :]
                gcarry_ref[ex] = gstage_ref[buf, tail, :]

        @pl.when(done > 0)
        def _():
            wait_all(1 - buf)

        for ex in range(e):
            wp = _window_start(base_ref, i, ex, e)
            hi = base_ref[(i + 1) * e + ex]
            needed = (hi - wp > q * m) | (q == 0)
            dst = jnp.where(needed, wp + q * m, cap)
            for cp in copies(buf, ex, pl.multiple_of(dst, SLOT_ALIGN)):
                cp.start()
        count_ref[0] = done + 1
        return 0

    lax.fori_loop(0, npass_ref[i], one_pass, 0)

    @pl.when(i == nc - 1)
    def _():
        wait_all((count_ref[0] - 1) % 2)
        stage_ref[0] = jnp.zeros(stage_ref.shape[1:], BF16)
        gstage_ref[0] = jnp.zeros(gstage_ref.shape[1:], F32)
        for ex in range(e):
            for cp in copies(0, ex, cap):
                cp.start()
        wait_all(0)


def _dispatch(base, npass, h2, rel3, gate3, cap):
    nc, e, _ = rel3.shape
    d = h2.shape[1]
    m = SLOT_WINDOW
    grid_spec = pltpu.PrefetchScalarGridSpec(
        num_scalar_prefetch=2,
        grid=(nc,),
        in_specs=[
            pl.BlockSpec((ROUTE_TILE, d), lambda i, *_: (i, 0)),
            pl.BlockSpec((1, e, ROUTE_TILE), lambda i, *_: (i, 0, 0)),
            pl.BlockSpec((1, e, ROUTE_TILE), lambda i, *_: (i, 0, 0)),
        ],
        out_specs=[pl.BlockSpec(memory_space=pl.ANY), pl.BlockSpec(memory_space=pl.ANY)],
        scratch_shapes=[
            pltpu.VMEM((2, e * m, d), BF16),
            pltpu.VMEM((2, e * m, 128), F32),
            pltpu.VMEM((e, SLOT_ALIGN, d), BF16),
            pltpu.VMEM((e, SLOT_ALIGN, 128), F32),
            pltpu.SMEM((1,), I32),
            pltpu.SemaphoreType.DMA((2,)),
        ],
    )
    return pl.pallas_call(
        functools.partial(_dispatch_kernel, e=e, cap=cap, nc=nc),
        grid_spec=grid_spec,
        out_shape=[
            jax.ShapeDtypeStruct((e, cap + m, d), BF16),
            jax.ShapeDtypeStruct((e, cap + m, 128), F32),
        ],
        compiler_params=pltpu.CompilerParams(
            dimension_semantics=("arbitrary",), vmem_limit_bytes=V7X_VMEM_LIMIT),
        name="dispatch",
    )(base, npass, h2, rel3, gate3)


def _ffn_kernel(xe_ref, gs_ref, wg_ref, wu_ref, wd_ref, y_ref, hid_ref, *, nf, fc):
    p = pl.program_id(2)

    @pl.when(p < nf)
    def _():
        xe = xe_ref[0]
        g = jnp.dot(xe, wg_ref[0], preferred_element_type=F32)
        u = jnp.dot(xe, wu_ref[0], preferred_element_type=F32)
        hid_ref[p] = (g * _sigmoid(g) * u).astype(BF16)

    @pl.when(p >= nf)
    def _():
        y = jnp.dot(hid_ref[0], wd_ref[0, 0:fc, :], preferred_element_type=F32)
        for f in range(1, nf):
            y = y + jnp.dot(hid_ref[f], wd_ref[0, f * fc:(f + 1) * fc, :],
                            preferred_element_type=F32)
        y_ref[0] = (y * gs_ref[0][:, 0:1]).astype(BF16)


def _ffn(xe, gs, wg, wu, wd, cap, rt, fc, oc):
    e, _, d = xe.shape
    dff = wg.shape[2]
    nf = dff // fc
    no = d // oc
    out_col = lambda p: jnp.maximum(p - nf, 0)
    return pl.pallas_call(
        functools.partial(_ffn_kernel, nf=nf, fc=fc),
        grid=(e, cap // rt, nf + no),
        in_specs=[
            pl.BlockSpec((1, rt, d), lambda ex, r, p: (ex, r, 0)),
            pl.BlockSpec((1, rt, 128), lambda ex, r, p: (ex, r, 0)),
            pl.BlockSpec((1, d, fc), lambda ex, r, p: (ex, 0, jnp.minimum(p, nf - 1))),
            pl.BlockSpec((1, d, fc), lambda ex, r, p: (ex, 0, jnp.minimum(p, nf - 1))),
            pl.BlockSpec((1, dff, oc), lambda ex, r, p: (ex, 0, out_col(p))),
        ],
        out_specs=pl.BlockSpec((1, rt, oc), lambda ex, r, p: (ex, r, out_col(p))),
        out_shape=jax.ShapeDtypeStruct((e, cap, d), BF16),
        scratch_shapes=[pltpu.VMEM((nf, rt, fc), BF16)],
        compiler_params=pltpu.CompilerParams(
            dimension_semantics=("parallel", "parallel", "arbitrary"),
            vmem_limit_bytes=V7X_VMEM_LIMIT),
        name="ffn",
    )(xe, gs, wg, wu, wd)


def _combine_kernel(base_ref, npass_ref, x1_ref, rel_ref, gf_ref, y_ref, out_ref,
                    ystage_ref, sem, *, e, cap, nc):
    m = SLOT_WINDOW
    i = pl.program_id(0)
    buf = i % 2
    jota = lax.broadcasted_iota(I32, (e, m, ROUTE_TILE), 1)
    eota = lax.broadcasted_iota(I32, (e, 1, 1), 0)

    def windows(tile, q, into):
        shift = jnp.zeros((e, 1, 1), I32)
        cps = []
        for ex in range(e):
            wp = _window_start(base_ref, tile, ex, e) + q * m
            rp = jnp.minimum(wp, cap - m)
            shift = jnp.where(eota == ex, wp - rp, shift)
            cps.append(pltpu.make_async_copy(
                y_ref.at[ex, pl.ds(pl.multiple_of(rp, SLOT_ALIGN), m)],
                ystage_ref.at[into, pl.ds(ex * m, m)], sem.at[into]))
        return cps, shift

    def expand(q, shift, acc):
        relq = (rel_ref[0] - q * m)[:, None, :]
        hit = (relq >= 0) & (relq < m) & (relq + shift == jota)
        p = hit.astype(F32).reshape(e * m, ROUTE_TILE).astype(BF16)
        return acc + lax.dot_general(p, ystage_ref[buf], (((0,), (0,)), ((), ())),
                                     preferred_element_type=F32)

    @pl.when(i == 0)
    def _():
        for cp in windows(0, 0, 0)[0]:
            cp.start()

    @pl.when(i + 1 < nc)
    def _():
        for cp in windows(i + 1, 0, 1 - buf)[0]:
            cp.start()

    cps, shift = windows(i, 0, buf)
    for cp in cps:
        cp.wait()
    acc = expand(0, shift, x1_ref[...])

    def extra_pass(q, acc):
        cps, shift = windows(i, q, buf)
        for cp in cps:
            cp.start()
        for cp in cps:
            cp.wait()
        return expand(q, shift, acc)

    x2 = lax.fori_loop(1, npass_ref[i], extra_pass, acc)
    out_ref[...] = _rms(x2) * gf_ref[...]


def _combine(base, npass, x1, rel3, gf, y, cap):
    nc, e, _ = rel3.shape
    n, d = x1.shape
    m = SLOT_WINDOW
    grid_spec = pltpu.PrefetchScalarGridSpec(
        num_scalar_prefetch=2,
        grid=(nc,),
        in_specs=[
            pl.BlockSpec((ROUTE_TILE, d), lambda i, *_: (i, 0)),
            pl.BlockSpec((1, e, ROUTE_TILE), lambda i, *_: (i, 0, 0)),
            pl.BlockSpec((1, d), lambda i, *_: (0, 0)),
            pl.BlockSpec(memory_space=pl.ANY),
        ],
        out_specs=pl.BlockSpec((ROUTE_TILE, d), lambda i, *_: (i, 0)),
        scratch_shapes=[
            pltpu.VMEM((2, e * m, d), BF16),
            pltpu.SemaphoreType.DMA((2,)),
        ],
    )
    return pl.pallas_call(
        functools.partial(_combine_kernel, e=e, cap=cap, nc=nc),
        grid_spec=grid_spec,
        out_shape=jax.ShapeDtypeStruct((n, d), F32),
        compiler_params=pltpu.CompilerParams(
            dimension_semantics=("arbitrary",), vmem_limit_bytes=V7X_VMEM_LIMIT),
        name="combine",
    )(base, npass, x1, rel3, gf, y)


def _largest_tile(total, limit, multiple):
    t = min(total, limit)
    while total % t or t % multiple:
        t -= 1
    return t


def _trunk(x, w, cap_factor, experts):
    b, s, d = x.shape
    n = b * s
    e = w["wrt"].shape[0]
    dc = w["cw"].shape[1]
    ds = w["sw"].shape[1]
    cap = cap_factor * n // e
    assert n % ROUTE_TILE == 0 and s % ROUTE_TILE == 0 and cap >= SLOT_WINDOW
    assert cap % SLOT_ALIGN == 0

    tm = _largest_tile(n, 256, 8)
    ts = _largest_tile(s, 256, ROUTE_TILE)
    flat = lambda v: v.reshape(-1, v.shape[-1])
    ride_in = [] if experts else [flat(w["wg32"]), flat(w["wu32"])]
    ride_mix = [] if experts else [flat(w["wd32"])]
    u, cast_in = _inproj(x.reshape(n, d), w["g1"], w["w_in"], dc, ds, tm, ride_in)
    (x1, h2, aff3), cast_mix = _mixer(
        u.reshape(b, s, dc + 2 * ds), x, w["cw"], w["cb"], w["lg"], w["lb"], w["sw"], w["sb"],
        w["goc"], w["gos"], w["w_out"], w["g2"], w["wrt"], ts, ride_mix)
    if not experts:
        experts = tuple(v.reshape(w["wg32"].shape[0], -1, v.shape[-1])
                        for v in (*cast_in, *cast_mix))
    wg, wu, wd = experts
    rel3, gate3, base3 = _route(aff3, cap)

    base = jnp.concatenate([base3[:, :, 0], jnp.full((1, e), cap, I32)], axis=0)
    rows = base[1:] - _align_down(base[:-1])
    npass = jnp.maximum((jnp.max(rows, axis=1) + SLOT_WINDOW - 1) // SLOT_WINDOW, 1).astype(I32)
    base = base.reshape(-1)

    xe, gs = _dispatch(base, npass, h2, rel3, gate3, cap)
    rt = _largest_tile(cap, 1024, SLOT_ALIGN)
    fc = _largest_tile(wg.shape[2], 1024, 128)
    oc = _largest_tile(d, 1024, 128)
    y = _ffn(xe, gs, wg, wu, wd, cap, rt, fc, oc)
    out = _combine(base, npass, x1.reshape(n, d), rel3, w["gf"], y, cap)
    return out.reshape(b, s, d), experts


def kernel(x_prompt, x_sample, g_norm1, w_in, conf_dw_w, conf_dw_b, conf_ln_g, conf_ln_b,
           sc_dw_w, sc_dw_b, g_out_conf, g_out_sc, w_out, g_norm2, w_router, w_gate, w_up,
           w_down, g_final):
    depth = w_in.shape[0]
    cap_factor = 2
    row = lambda v: v.reshape(1, -1)
    layers = []
    for l in range(depth):
        layers.append(dict(
            g1=row(g_norm1[l]), w_in=w_in[l].astype(BF16),
            cw=conf_dw_w[l], cb=row(conf_dw_b[l]), lg=row(conf_ln_g[l]), lb=row(conf_ln_b[l]),
            sw=sc_dw_w[l], sb=row(sc_dw_b[l]), goc=row(g_out_conf[l]), gos=row(g_out_sc[l]),
            w_out=w_out[l].astype(BF16), g2=row(g_norm2[l]),
            wrt=w_router[l].T.astype(BF16),
            wg32=w_gate[l], wu32=w_up[l], wd32=w_down[l],
        ))
    assert depth == 1, "the final norm is fused into the last layer's combine"
    outs = []
    experts = None
    for x in (x_prompt, x_sample):
        w = dict(layers[0], gf=row(g_final))
        out, experts = _trunk(x, w, cap_factor, experts)
        outs.append(out)
    return tuple(outs)
```

```python
import functools

import jax
import jax.numpy as jnp
from jax import lax
from jax.experimental import pallas as pl
from jax.experimental.pallas import tpu as pltpu

F32 = jnp.float32
BF16 = jnp.bfloat16
I32 = jnp.int32

EPS = 1e-6
CONF_HALO = 16
SC_HALO = 8
SUBLANES = 8
CONV_ROWS = 32
ROUTE_TILE = 256
SLOT_WINDOW = 64
SLOT_ALIGN = 16
UNSELECTED = -(1 << 20)
V7X_VMEM_LIMIT = 56 * 1024 * 1024


def _sigmoid(x):
    return 1.0 / (1.0 + jnp.exp(-x))


def _rms(x):
    return x * lax.rsqrt(jnp.mean(x * x, axis=-1, keepdims=True) + EPS)


def _rider_specs(riders, steps, step_of):
    in_specs, out_specs, out_shapes = [], [], []
    for w in riders:
        rows, cols = w.shape
        assert rows % (steps * SLOT_ALIGN) == 0, (rows, steps)
        spec = pl.BlockSpec((rows // steps, cols), lambda *g: (step_of(*g), 0))
        in_specs.append(spec)
        out_specs.append(spec)
        out_shapes.append(jax.ShapeDtypeStruct(w.shape, BF16))
    return in_specs, out_specs, out_shapes


def _cast_riders(srcs, dsts):
    for src, dst in zip(srcs, dsts):
        dst[...] = src[...].astype(BF16)


def _inproj_kernel(*refs, dc, ds, nride):
    x_ref, g_ref, w_ref = refs[:3]
    u_ref = refs[3 + nride]
    _cast_riders(refs[3:3 + nride], refs[4 + nride:])
    h = (_rms(x_ref[...]) * g_ref[...]).astype(BF16)

    def proj(c0, width):
        return jnp.dot(h, w_ref[:, c0:c0 + width], preferred_element_type=F32)

    u_ref[:, 0:dc] = proj(0, dc) * _sigmoid(proj(dc, dc))
    cg = proj(2 * dc + ds, ds)
    u_ref[:, dc:dc + ds] = cg * proj(2 * dc + 2 * ds, ds)
    u_ref[:, dc + ds:dc + 2 * ds] = proj(2 * dc, ds)


def _inproj(x2, g1, w_in, dc, ds, tm, riders=()):
    n, d = x2.shape
    cols = w_in.shape[1]
    r_in, r_out, r_shapes = _rider_specs(riders, n // tm, lambda i: i)
    outs = pl.pallas_call(
        functools.partial(_inproj_kernel, dc=dc, ds=ds, nride=len(riders)),
        grid=(n // tm,),
        in_specs=[
            pl.BlockSpec((tm, d), lambda i: (i, 0)),
            pl.BlockSpec((1, d), lambda i: (0, 0)),
            pl.BlockSpec((d, cols), lambda i: (0, 0), pipeline_mode=pl.Buffered(1)),
        ] + r_in,
        out_specs=[pl.BlockSpec((tm, dc + 2 * ds), lambda i: (i, 0))] + r_out,
        out_shape=[jax.ShapeDtypeStruct((n, dc + 2 * ds), F32)] + r_shapes,
        compiler_params=pltpu.CompilerParams(
            dimension_semantics=("parallel",), vmem_limit_bytes=V7X_VMEM_LIMIT),
        name="inproj",
    )(x2, g1, w_in, *riders)
    return outs[0], outs[1:]


def _sc_shifts(ks):
    pad = (ks - 1) // 2
    return sorted({(SC_HALO + k - pad) % SUBLANES for k in range(ks)} - {0})


MIXER_INPUTS = 17


def _mixer_kernel(*refs, ts, dc, ds, nt, kc, ks, nride):
    (u_ref, ap_ref, an_ref, cp_ref, cn_ref, x_ref,
     cw_ref, cb_ref, lg_ref, lb_ref, sw_ref, sb_ref, goc_ref, gos_ref,
     wout_ref, g2_ref, wrt_ref) = refs[:MIXER_INPUTS]
    outs = refs[MIXER_INPUTS + nride:]
    x1_ref, h2_ref, aff_ref = outs[:3]
    aext_ref, cext_ref, ash_ref, csh_ref, ya_ref, ys_ref = outs[3 + nride:]
    _cast_riders(refs[MIXER_INPUTS:MIXER_INPUTS + nride], outs[3:3 + nride])
    t = pl.program_id(1)
    keep_prev = (t > 0).astype(F32)
    keep_next = (t < nt - 1).astype(F32)
    aext_ref[0:CONF_HALO, :] = ap_ref[0] * keep_prev
    aext_ref[CONF_HALO:CONF_HALO + ts, :] = u_ref[0, :, 0:dc]
    aext_ref[CONF_HALO + ts:2 * CONF_HALO + ts, :] = an_ref[0] * keep_next
    cext_ref[0:SC_HALO, :] = cp_ref[0] * keep_prev
    cext_ref[SC_HALO:SC_HALO + ts, :] = u_ref[0, :, dc:dc + ds]
    cext_ref[SC_HALO + ts:2 * SC_HALO + ts, :] = cn_ref[0] * keep_next

    pad_c = (kc - 1) // 2
    pad_s = (ks - 1) // 2
    for r in range(1, SUBLANES):
        ash_ref[r - 1] = aext_ref[r:r + ash_ref.shape[1], :]
    sc_shifts = _sc_shifts(ks)
    for slot, r in enumerate(sc_shifts):
        csh_ref[slot] = cext_ref[r:r + csh_ref.shape[1], :]

    groups = CONV_ROWS // SUBLANES

    def grouped(v):
        return v.reshape(groups, SUBLANES, v.shape[-1])

    def shifted(ext_ref, sh_ref, o, index):
        q, r = divmod(o, SUBLANES)
        rows = pl.ds(q * SUBLANES, CONV_ROWS)
        return grouped(ext_ref[rows, :] if r == 0 else sh_ref[index(r), rows, :])

    for c in range(ts // CONV_ROWS):
        r0 = c * CONV_ROWS
        conv = jnp.broadcast_to(sb_ref[...][None], (groups, SUBLANES, ds))
        for k in range(ks):
            tap = shifted(cext_ref, csh_ref, r0 + SC_HALO + k - pad_s, sc_shifts.index)
            conv = conv + sw_ref[k][None] * tap
        s = grouped(u_ref[0, r0:r0 + CONV_ROWS, dc + ds:dc + 2 * ds]) * conv
        ys_ref[r0:r0 + CONV_ROWS, :] = (
            _rms(s) * gos_ref[...][None]).reshape(CONV_ROWS, ds).astype(BF16)
    x1 = x_ref[0] + jnp.dot(ys_ref[...], wout_ref[dc:dc + ds, :], preferred_element_type=F32)

    for c in range(ts // CONV_ROWS):
        r0 = c * CONV_ROWS
        acc = jnp.broadcast_to(cb_ref[...][None], (groups, SUBLANES, dc))
        for k in range(kc):
            tap = shifted(aext_ref, ash_ref, r0 + CONF_HALO + k - pad_c, lambda r: r - 1)
            acc = acc + cw_ref[k][None] * tap
        mu = jnp.mean(acc, axis=-1, keepdims=True)
        cen = acc - mu
        var = jnp.mean(cen * cen, axis=-1, keepdims=True)
        a = cen * lax.rsqrt(var + EPS) * lg_ref[...][None] + lb_ref[...][None]
        a = a * _sigmoid(a)
        ya_ref[r0:r0 + CONV_ROWS, :] = (
            _rms(a) * goc_ref[...][None]).reshape(CONV_ROWS, dc).astype(BF16)
    x1 = x1 + jnp.dot(ya_ref[...], wout_ref[0:dc, :], preferred_element_type=F32)
    x1_ref[0] = x1
    h2 = (_rms(x1) * g2_ref[...]).astype(BF16)
    h2_ref[...] = h2
    logits = lax.dot_general(wrt_ref[...], h2, (((1,), (1,)), ((), ())),
                             preferred_element_type=F32)
    ex = jnp.exp(logits - jnp.max(logits, axis=0, keepdims=True))
    aff = ex / jnp.sum(ex, axis=0, keepdims=True)
    for j in range(ts // ROUTE_TILE):
        aff_ref[j] = aff[:, j * ROUTE_TILE:(j + 1) * ROUTE_TILE]


def _replicate_rows(v):
    return jnp.broadcast_to(v, (SUBLANES, v.shape[1]))


def _replicate_taps(w):
    return jnp.broadcast_to(w[:, None, :], (w.shape[0], SUBLANES, w.shape[1]))


def _mixer(u3, x3, cw, cb, lg, lb, sw, sb, goc, gos, wout, g2, wrt, ts, riders=()):
    b, s, d = x3.shape
    dc = cw.shape[1]
    ds = sw.shape[1]
    kc = cw.shape[0]
    ks = sw.shape[0]
    e = wrt.shape[0]
    nt = s // ts
    n = b * s
    hb_c = ts // CONF_HALO
    hb_s = ts // SC_HALO
    col_cv = dc // ds

    def const(shape):
        return pl.BlockSpec(shape, lambda i, t: (0,) * len(shape))

    in_specs = [
        pl.BlockSpec((1, ts, dc + 2 * ds), lambda i, t: (i, t, 0)),
        pl.BlockSpec((1, CONF_HALO, dc), lambda i, t: (i, jnp.maximum(t * hb_c - 1, 0), 0)),
        pl.BlockSpec((1, CONF_HALO, dc),
                     lambda i, t: (i, jnp.minimum((t + 1) * hb_c, s // CONF_HALO - 1), 0)),
        pl.BlockSpec((1, SC_HALO, ds),
                     lambda i, t: (i, jnp.maximum(t * hb_s - 1, 0), col_cv)),
        pl.BlockSpec((1, SC_HALO, ds),
                     lambda i, t: (i, jnp.minimum((t + 1) * hb_s, s // SC_HALO - 1), col_cv)),
        pl.BlockSpec((1, ts, d), lambda i, t: (i, t, 0)),
        const((kc, SUBLANES, dc)), const((SUBLANES, dc)), const((SUBLANES, dc)),
        const((SUBLANES, dc)),
        const((ks, SUBLANES, ds)), const((SUBLANES, ds)), const((SUBLANES, dc)),
        const((SUBLANES, ds)),
        pl.BlockSpec((dc + ds, d), lambda i, t: (0, 0), pipeline_mode=pl.Buffered(1)),
        const((1, d)), const((e, d)),
    ]
    out_specs = [
        pl.BlockSpec((1, ts, d), lambda i, t: (i, t, 0)),
        pl.BlockSpec((ts, d), lambda i, t: (i * nt + t, 0)),
        pl.BlockSpec((ts // ROUTE_TILE, e, ROUTE_TILE), lambda i, t: (i * nt + t, 0, 0)),
    ]
    out_shape = [
        jax.ShapeDtypeStruct((b, s, d), F32),
        jax.ShapeDtypeStruct((n, d), BF16),
        jax.ShapeDtypeStruct((n // ROUTE_TILE, e, ROUTE_TILE), F32),
    ]
    assert len(in_specs) == MIXER_INPUTS
    r_in, r_out, r_shapes = _rider_specs(riders, b * nt, lambda i, t: i * nt + t)
    outs = pl.pallas_call(
        functools.partial(_mixer_kernel, ts=ts, dc=dc, ds=ds, nt=nt, kc=kc, ks=ks,
                          nride=len(riders)),
        grid=(b, nt),
        in_specs=in_specs + r_in,
        out_specs=out_specs + r_out,
        out_shape=out_shape + r_shapes,
        scratch_shapes=[
            pltpu.VMEM((ts + 2 * CONF_HALO, dc), F32),
            pltpu.VMEM((ts + 2 * SC_HALO, ds), F32),
            pltpu.VMEM((SUBLANES - 1, ts + 2 * CONF_HALO - SUBLANES, dc), F32),
            pltpu.VMEM((len(_sc_shifts(ks)), ts + 2 * SC_HALO - SUBLANES, ds), F32),
            pltpu.VMEM((ts, dc), BF16),
            pltpu.VMEM((ts, ds), BF16),
        ],
        compiler_params=pltpu.CompilerParams(
            dimension_semantics=("parallel", "parallel"), vmem_limit_bytes=V7X_VMEM_LIMIT),
        name="mixer",
    )(u3, u3, u3, u3, u3, x3, _replicate_taps(cw), _replicate_rows(cb), _replicate_rows(lg),
      _replicate_rows(lb), _replicate_taps(sw), _replicate_rows(sb), _replicate_rows(goc),
      _replicate_rows(gos), wout, g2, wrt, *riders)
    return outs[:3], outs[3:]


def _route_kernel(aff_ref, rel_ref, gate_ref, base_ref, before_ref, *, cap, nc, e):
    capf = jnp.float32(cap)

    def count(pred):
        bits = pltpu.bitcast(aff_ref[...], I32)
        per_lane = jnp.sum(pred(bits).astype(F32), axis=0)
        return jnp.sum(per_lane, axis=-1, keepdims=True)

    def search(i, cur):
        cand = cur | lax.shift_left(jnp.int32(1), 30 - i)
        return jnp.where(count(lambda bits: bits >= cand[None]) >= capf, cand, cur)

    thr = lax.fori_loop(0, 31, search, jnp.zeros((e, 1), I32))
    need_eq = capf - count(lambda bits: bits > thr[None])

    row = lax.broadcasted_iota(I32, (ROUTE_TILE, ROUTE_TILE), 0)
    col = lax.broadcasted_iota(I32, (ROUTE_TILE, ROUTE_TILE), 1)
    tri = (row <= col).astype(BF16)

    def rank(flags):
        within = jnp.dot(flags.reshape(nc * e, ROUTE_TILE).astype(BF16), tri,
                         preferred_element_type=F32).reshape(nc, e, ROUTE_TILE)
        before_ref[...] = jnp.broadcast_to(within[:, :, ROUTE_TILE - 1:], (nc, e, 128))

        def tile_prefix(c, running):
            total = before_ref[c]
            before_ref[c] = running
            return running + total

        lax.fori_loop(0, nc, tile_prefix, jnp.zeros((e, 128), F32))
        before = before_ref[...][:, :, 0:1]
        return within - flags + before, before

    aff = aff_ref[...]
    bits = pltpu.bitcast(aff, I32)
    eq = bits == thr[None]
    eq_rank, _ = rank(eq.astype(F32))
    sel = (bits > thr[None]) | (eq & (eq_rank < need_eq[None]))
    slot, sel_before = rank(sel.astype(F32))
    window = jnp.floor(sel_before * (1.0 / SLOT_ALIGN)) * SLOT_ALIGN
    rel_ref[...] = jnp.where(sel, (slot - window).astype(I32), UNSELECTED)
    gate_ref[...] = jnp.where(sel, aff, 0.0)
    base_ref[...] = before_ref[...].astype(I32)


def _route(aff3, cap):
    nc, e, _ = aff3.shape
    full = lambda shape: pl.BlockSpec(shape, lambda: (0,) * len(shape))
    return pl.pallas_call(
        functools.partial(_route_kernel, cap=cap, nc=nc, e=e),
        in_specs=[full((nc, e, ROUTE_TILE))],
        out_specs=[full((nc, e, ROUTE_TILE)), full((nc, e, ROUTE_TILE)), full((nc, e, 128))],
        out_shape=[
            jax.ShapeDtypeStruct((nc, e, ROUTE_TILE), I32),
            jax.ShapeDtypeStruct((nc, e, ROUTE_TILE), F32),
            jax.ShapeDtypeStruct((nc, e, 128), I32),
        ],
        scratch_shapes=[pltpu.VMEM((nc, e, 128), F32)],
        compiler_params=pltpu.CompilerParams(vmem_limit_bytes=V7X_VMEM_LIMIT),
        name="route",
    )(aff3)


def _align_down(slot):
    shift = SLOT_ALIGN.bit_length() - 1
    return (slot >> shift) << shift


def _window_start(base_ref, i, ex, e):
    return _align_down(base_ref[i * e + ex])


def _dispatch_kernel(base_ref, npass_ref, h2_ref, rel_ref, gate_ref,
                     xe_ref, gs_ref,
                     stage_ref, gstage_ref, carry_ref, gcarry_ref, count_ref, sem,
                     *, e, cap, nc):
    m = SLOT_WINDOW
    i = pl.program_id(0)

    @pl.when(i == 0)
    def _():
        carry_ref[...] = jnp.zeros_like(carry_ref)
        gcarry_ref[...] = jnp.zeros_like(gcarry_ref)
        count_ref[0] = 0

    def copies(buf, ex, dst_row):
        rows = pl.ds(ex * m, m)
        return (
            pltpu.make_async_copy(stage_ref.at[buf, rows], xe_ref.at[ex, pl.ds(dst_row, m)],
                                  sem.at[0]),
            pltpu.make_async_copy(gstage_ref.at[buf, rows], gs_ref.at[ex, pl.ds(dst_row, m)],
                                  sem.at[1]),
        )

    def wait_all(buf):
        for ex in range(e):
            for cp in copies(buf, ex, 0):
                cp.wait()

    h2 = h2_ref[...]
    jota = lax.broadcasted_iota(I32, (e, m, ROUTE_TILE), 1)

    def one_pass(q, _):
        done = count_ref[0]
        buf = done % 2
        relq = rel_ref[0] - q * m
        hit = relq[:, None, :] == jota
        p = hit.astype(F32)
        stage_ref[buf] = jnp.dot(p.reshape(e * m, ROUTE_TILE).astype(BF16), h2,
                                 preferred_element_type=F32).astype(BF16)
        gsel = jnp.sum(p * gate_ref[0][:, None, :], axis=-1, keepdims=True)
        gstage_ref[buf] = jnp.broadcast_to(gsel, (e, m, 128)).reshape(e * m, 128)

        for ex in range(e):
            head = pl.ds(ex * m, SLOT_ALIGN)
            wp = _window_start(base_ref, i, ex, e)
            hi = base_ref[(i + 1) * e + ex]

            @pl.when(q == 0)
            def _():
                stage_ref[buf, head, :] = stage_ref[buf, head, :] + carry_ref[ex]
                gstage_ref[buf, head, :] = gstage_ref[buf, head, :] + gcarry_ref[ex]
                carry_ref[ex] = jnp.zeros((SLOT_ALIGN, carry_ref.shape[-1]), BF16)
                gcarry_ref[ex] = jnp.zeros((SLOT_ALIGN, 128), F32)

            off = _align_down(hi) - wp - q * m

            @pl.when((off >= 0) & (off < m))
            def _():
                tail = pl.ds(pl.multiple_of(ex * m + off, SLOT_ALIGN), SLOT_ALIGN)
                carry_ref[ex] = stage_ref[buf, tail, :]
                gcarry_ref[ex] = gstage_ref[buf, tail, :]

        @pl.when(done > 0)
        def _():
            wait_all(1 - buf)

        for ex in range(e):
            wp = _window_start(base_ref, i, ex, e)
            hi = base_ref[(i + 1) * e + ex]
            needed = (hi - wp > q * m) | (q == 0)
            dst = jnp.where(needed, wp + q * m, cap)
            for cp in copies(buf, ex, pl.multiple_of(dst, SLOT_ALIGN)):
                cp.start()
        count_ref[0] = done + 1
        return 0

    lax.fori_loop(0, npass_ref[i], one_pass, 0)

    @pl.when(i == nc - 1)
    def _():
        wait_all((count_ref[0] - 1) % 2)
        stage_ref[0] = jnp.zeros(stage_ref.shape[1:], BF16)
        gstage_ref[0] = jnp.zeros(gstage_ref.shape[1:], F32)
        for ex in range(e):
            for cp in copies(0, ex, cap):
                cp.start()
        wait_all(0)


def _dispatch(base, npass, h2, rel3, gate3, cap):
    nc, e, _ = rel3.shape
    d = h2.shape[1]
    m = SLOT_WINDOW
    grid_spec = pltpu.PrefetchScalarGridSpec(
        num_scalar_prefetch=2,
        grid=(nc,),
        in_specs=[
            pl.BlockSpec((ROUTE_TILE, d), lambda i, *_: (i, 0)),
            pl.BlockSpec((1, e, ROUTE_TILE), lambda i, *_: (i, 0, 0)),
            pl.BlockSpec((1, e, ROUTE_TILE), lambda i, *_: (i, 0, 0)),
        ],
        out_specs=[pl.BlockSpec(memory_space=pl.ANY), pl.BlockSpec(memory_space=pl.ANY)],
        scratch_shapes=[
            pltpu.VMEM((2, e * m, d), BF16),
            pltpu.VMEM((2, e * m, 128), F32),
            pltpu.VMEM((e, SLOT_ALIGN, d), BF16),
            pltpu.VMEM((e, SLOT_ALIGN, 128), F32),
            pltpu.SMEM((1,), I32),
            pltpu.SemaphoreType.DMA((2,)),
        ],
    )
    return pl.pallas_call(
        functools.partial(_dispatch_kernel, e=e, cap=cap, nc=nc),
        grid_spec=grid_spec,
        out_shape=[
            jax.ShapeDtypeStruct((e, cap + m, d), BF16),
            jax.ShapeDtypeStruct((e, cap + m, 128), F32),
        ],
        compiler_params=pltpu.CompilerParams(
            dimension_semantics=("arbitrary",), vmem_limit_bytes=V7X_VMEM_LIMIT),
        name="dispatch",
    )(base, npass, h2, rel3, gate3)


def _ffn_kernel(xe_ref, gs_ref, wg_ref, wu_ref, wd_ref, y_ref, hid_ref, *, nf, fc):
    p = pl.program_id(2)

    @pl.when(p < nf)
    def _():
        xe = xe_ref[0]
        g = jnp.dot(xe, wg_ref[0], preferred_element_type=F32)
        u = jnp.dot(xe, wu_ref[0], preferred_element_type=F32)
        hid_ref[p] = (g * _sigmoid(g) * u).astype(BF16)

    @pl.when(p >= nf)
    def _():
        y = jnp.dot(hid_ref[0], wd_ref[0, 0:fc, :], preferred_element_type=F32)
        for f in range(1, nf):
            y = y + jnp.dot(hid_ref[f], wd_ref[0, f * fc:(f + 1) * fc, :],
                            preferred_element_type=F32)
        y_ref[0] = (y * gs_ref[0][:, 0:1]).astype(BF16)


def _ffn(xe, gs, wg, wu, wd, cap, rt, fc, oc):
    e, _, d = xe.shape
    dff = wg.shape[2]
    nf = dff // fc
    no = d // oc
    out_col = lambda p: jnp.maximum(p - nf, 0)
    return pl.pallas_call(
        functools.partial(_ffn_kernel, nf=nf, fc=fc),
        grid=(e, cap // rt, nf + no),
        in_specs=[
            pl.BlockSpec((1, rt, d), lambda ex, r, p: (ex, r, 0)),
            pl.BlockSpec((1, rt, 128), lambda ex, r, p: (ex, r, 0)),
            pl.BlockSpec((1, d, fc), lambda ex, r, p: (ex, 0, jnp.minimum(p, nf - 1))),
            pl.BlockSpec((1, d, fc), lambda ex, r, p: (ex, 0, jnp.minimum(p, nf - 1))),
            pl.BlockSpec((1, dff, oc), lambda ex, r, p: (ex, 0, out_col(p))),
        ],
        out_specs=pl.BlockSpec((1, rt, oc), lambda ex, r, p: (ex, r, out_col(p))),
        out_shape=jax.ShapeDtypeStruct((e, cap, d), BF16),
        scratch_shapes=[pltpu.VMEM((nf, rt, fc), BF16)],
        compiler_params=pltpu.CompilerParams(
            dimension_semantics=("parallel", "parallel", "arbitrary"),
            vmem_limit_bytes=V7X_VMEM_LIMIT),
        name="ffn",
    )(xe, gs, wg, wu, wd)


def _combine_kernel(base_ref, npass_ref, x1_ref, rel_ref, gf_ref, y_ref, out_ref,
                    ystage_ref, sem, *, e, cap, nc):
    m = SLOT_WINDOW
    i = pl.program_id(0)
    buf = i % 2
    jota = lax.broadcasted_iota(I32, (e, m, ROUTE_TILE), 1)
    eota = lax.broadcasted_iota(I32, (e, 1, 1), 0)

    def windows(tile, q, into):
        shift = jnp.zeros((e, 1, 1), I32)
        cps = []
        for ex in range(e):
            wp = _window_start(base_ref, tile, ex, e) + q * m
            rp = jnp.minimum(wp, cap - m)
            shift = jnp.where(eota == ex, wp - rp, shift)
            cps.append(pltpu.make_async_copy(
                y_ref.at[ex, pl.ds(pl.multiple_of(rp, SLOT_ALIGN), m)],
                ystage_ref.at[into, pl.ds(ex * m, m)], sem.at[into]))
        return cps, shift

    def expand(q, shift, acc):
        relq = (rel_ref[0] - q * m)[:, None, :]
        hit = (relq >= 0) & (relq < m) & (relq + shift == jota)
        p = hit.astype(F32).reshape(e * m, ROUTE_TILE).astype(BF16)
        return acc + lax.dot_general(p, ystage_ref[buf], (((0,), (0,)), ((), ())),
                                     preferred_element_type=F32)

    @pl.when(i == 0)
    def _():
        for cp in windows(0, 0, 0)[0]:
            cp.start()

    @pl.when(i + 1 < nc)
    def _():
        for cp in windows(i + 1, 0, 1 - buf)[0]:
            cp.start()

    cps, shift = windows(i, 0, buf)
    for cp in cps:
        cp.wait()
    acc = expand(0, shift, x1_ref[...])

    def extra_pass(q, acc):
        cps, shift = windows(i, q, buf)
        for cp in cps:
            cp.start()
        for cp in cps:
            cp.wait()
        return expand(q, shift, acc)

    x2 = lax.fori_loop(1, npass_ref[i], extra_pass, acc)
    out_ref[...] = _rms(x2) * gf_ref[...]


def _combine(base, npass, x1, rel3, gf, y, cap):
    nc, e, _ = rel3.shape
    n, d = x1.shape
    m = SLOT_WINDOW
    grid_spec = pltpu.PrefetchScalarGridSpec(
        num_scalar_prefetch=2,
        grid=(nc,),
        in_specs=[
            pl.BlockSpec((ROUTE_TILE, d), lambda i, *_: (i, 0)),
            pl.BlockSpec((1, e, ROUTE_TILE), lambda i, *_: (i, 0, 0)),
            pl.BlockSpec((1, d), lambda i, *_: (0, 0)),
            pl.BlockSpec(memory_space=pl.ANY),
        ],
        out_specs=pl.BlockSpec((ROUTE_TILE, d), lambda i, *_: (i, 0)),
        scratch_shapes=[
            pltpu.VMEM((2, e * m, d), BF16),
            pltpu.SemaphoreType.DMA((2,)),
        ],
    )
    return pl.pallas_call(
        functools.partial(_combine_kernel, e=e, cap=cap, nc=nc),
        grid_spec=grid_spec,
        out_shape=jax.ShapeDtypeStruct((n, d), F32),
        compiler_params=pltpu.CompilerParams(
            dimension_semantics=("arbitrary",), vmem_limit_bytes=V7X_VMEM_LIMIT),
        name="combine",
    )(base, npass, x1, rel3, gf, y)


def _largest_tile(total, limit, multiple):
    t = min(total, limit)
    while total % t or t % multiple:
        t -= 1
    return t


def _trunk(x, w, cap_factor, experts):
    b, s, d = x.shape
    n = b * s
    e = w["wrt"].shape[0]
    dc = w["cw"].shape[1]
    ds = w["sw"].shape[1]
    cap = cap_factor * n // e
    assert n % ROUTE_TILE == 0 and s % ROUTE_TILE == 0 and cap >= SLOT_WINDOW
    assert cap % SLOT_ALIGN == 0

    tm = _largest_tile(n, 256, 8)
    ts = _largest_tile(s, 256, ROUTE_TILE)
    flat = lambda v: v.reshape(-1, v.shape[-1])
    ride_in = [] if experts else [flat(w["wg32"]), flat(w["wu32"])]
    ride_mix = [] if experts else [flat(w["wd32"])]
    u, cast_in = _inproj(x.reshape(n, d), w["g1"], w["w_in"], dc, ds, tm, ride_in)
    (x1, h2, aff3), cast_mix = _mixer(
        u.reshape(b, s, dc + 2 * ds), x, w["cw"], w["cb"], w["lg"], w["lb"], w["sw"], w["sb"],
        w["goc"], w["gos"], w["w_out"], w["g2"], w["wrt"], ts, ride_mix)
    if not experts:
        experts = tuple(v.reshape(w["wg32"].shape[0], -1, v.shape[-1])
                        for v in (*cast_in, *cast_mix))
    wg, wu, wd = experts
    rel3, gate3, base3 = _route(aff3, cap)

    base = jnp.concatenate([base3[:, :, 0], jnp.full((1, e), cap, I32)], axis=0)
    rows = base[1:] - _align_down(base[:-1])
    npass = jnp.maximum((jnp.max(rows, axis=1) + SLOT_WINDOW - 1) // SLOT_WINDOW, 1).astype(I32)
    base = base.reshape(-1)

    xe, gs = _dispatch(base, npass, h2, rel3, gate3, cap)
    rt = _largest_tile(cap, 1024, SLOT_ALIGN)
    fc = _largest_tile(wg.shape[2], 512, 128)
    oc = _largest_tile(d, 1024, 128)
    y = _ffn(xe, gs, wg, wu, wd, cap, rt, fc, oc)
    out = _combine(base, npass, x1.reshape(n, d), rel3, w["gf"], y, cap)
    return out.reshape(b, s, d), experts


def kernel(x_prompt, x_sample, g_norm1, w_in, conf_dw_w, conf_dw_b, conf_ln_g, conf_ln_b,
           sc_dw_w, sc_dw_b, g_out_conf, g_out_sc, w_out, g_norm2, w_router, w_gate, w_up,
           w_down, g_final):
    depth = w_in.shape[0]
    cap_factor = 2
    row = lambda v: v.reshape(1, -1)
    layers = []
    for l in range(depth):
        layers.append(dict(
            g1=row(g_norm1[l]), w_in=w_in[l].astype(BF16),
            cw=conf_dw_w[l], cb=row(conf_dw_b[l]), lg=row(conf_ln_g[l]), lb=row(conf_ln_b[l]),
            sw=sc_dw_w[l], sb=row(sc_dw_b[l]), goc=row(g_out_conf[l]), gos=row(g_out_sc[l]),
            w_out=w_out[l].astype(BF16), g2=row(g_norm2[l]),
            wrt=w_router[l].T.astype(BF16),
            wg32=w_gate[l], wu32=w_up[l], wd32=w_down[l],
        ))
    assert depth == 1, "the final norm is fused into the last layer's combine"
    outs = []
    experts = None
    for x in (x_prompt, x_sample):
        w = dict(layers[0], gf=row(g_final))
        out, experts = _trunk(x, w, cap_factor, experts)
        outs.append(out)
    return tuple(outs)
```

```python
import functools

import jax
import jax.numpy as jnp
from jax import lax
from jax.experimental import pallas as pl
from jax.experimental.pallas import tpu as pltpu

F32 = jnp.float32
BF16 = jnp.bfloat16
I32 = jnp.int32

EPS = 1e-6
CONF_HALO = 16
SC_HALO = 8
SUBLANES = 8
CONV_ROWS = 32
ROUTE_TILE = 256
SLOT_WINDOW = 64
SLOT_ALIGN = 16
UNSELECTED = -(1 << 20)
V7X_VMEM_LIMIT = 56 * 1024 * 1024


def _sigmoid(x):
    return 1.0 / (1.0 + jnp.exp(-x))


def _rms(x):
    return x * lax.rsqrt(jnp.mean(x * x, axis=-1, keepdims=True) + EPS)


def _rider_specs(riders, steps, step_of):
    in_specs, out_specs, out_shapes = [], [], []
    for w in riders:
        rows, cols = w.shape
        assert rows % (steps * SLOT_ALIGN) == 0, (rows, steps)
        spec = pl.BlockSpec((rows // steps, cols), lambda *g: (step_of(*g), 0))
        in_specs.append(spec)
        out_specs.append(spec)
        out_shapes.append(jax.ShapeDtypeStruct(w.shape, BF16))
    return in_specs, out_specs, out_shapes


def _cast_riders(srcs, dsts):
    for src, dst in zip(srcs, dsts):
        dst[...] = src[...].astype(BF16)


def _inproj_kernel(*refs, dc, ds, nride):
    x_ref, g_ref, w_ref = refs[:3]
    u_ref = refs[3 + nride]
    _cast_riders(refs[3:3 + nride], refs[4 + nride:])
    h = (_rms(x_ref[...]) * g_ref[...]).astype(BF16)

    def proj(c0, width):
        return jnp.dot(h, w_ref[:, c0:c0 + width], preferred_element_type=F32)

    u_ref[:, 0:dc] = proj(0, dc) * _sigmoid(proj(dc, dc))
    cg = proj(2 * dc + ds, ds)
    u_ref[:, dc:dc + ds] = cg * proj(2 * dc + 2 * ds, ds)
    u_ref[:, dc + ds:dc + 2 * ds] = proj(2 * dc, ds)


def _inproj(x2, g1, w_in, dc, ds, tm, riders=()):
    n, d = x2.shape
    cols = w_in.shape[1]
    r_in, r_out, r_shapes = _rider_specs(riders, n // tm, lambda i: i)
    outs = pl.pallas_call(
        functools.partial(_inproj_kernel, dc=dc, ds=ds, nride=len(riders)),
        grid=(n // tm,),
        in_specs=[
            pl.BlockSpec((tm, d), lambda i: (i, 0)),
            pl.BlockSpec((1, d), lambda i: (0, 0)),
            pl.BlockSpec((d, cols), lambda i: (0, 0), pipeline_mode=pl.Buffered(1)),
        ] + r_in,
        out_specs=[pl.BlockSpec((tm, dc + 2 * ds), lambda i: (i, 0))] + r_out,
        out_shape=[jax.ShapeDtypeStruct((n, dc + 2 * ds), F32)] + r_shapes,
        compiler_params=pltpu.CompilerParams(
            dimension_semantics=("parallel",), vmem_limit_bytes=V7X_VMEM_LIMIT),
        name="inproj",
    )(x2, g1, w_in, *riders)
    return outs[0], outs[1:]


def _sc_shifts(ks):
    pad = (ks - 1) // 2
    return sorted({(SC_HALO + k - pad) % SUBLANES for k in range(ks)} - {0})


MIXER_INPUTS = 17


def _mixer_kernel(*refs, ts, dc, ds, nt, kc, ks, nride):
    (u_ref, ap_ref, an_ref, cp_ref, cn_ref, x_ref,
     cw_ref, cb_ref, lg_ref, lb_ref, sw_ref, sb_ref, goc_ref, gos_ref,
     wout_ref, g2_ref, wrt_ref) = refs[:MIXER_INPUTS]
    outs = refs[MIXER_INPUTS + nride:]
    x1_ref, h2_ref, aff_ref = outs[:3]
    aext_ref, cext_ref, ash_ref, csh_ref, ya_ref, ys_ref = outs[3 + nride:]
    _cast_riders(refs[MIXER_INPUTS:MIXER_INPUTS + nride], outs[3:3 + nride])
    t = pl.program_id(1)
    keep_prev = (t > 0).astype(F32)
    keep_next = (t < nt - 1).astype(F32)
    aext_ref[0:CONF_HALO, :] = ap_ref[0] * keep_prev
    aext_ref[CONF_HALO:CONF_HALO + ts, :] = u_ref[0, :, 0:dc]
    aext_ref[CONF_HALO + ts:2 * CONF_HALO + ts, :] = an_ref[0] * keep_next
    cext_ref[0:SC_HALO, :] = cp_ref[0] * keep_prev
    cext_ref[SC_HALO:SC_HALO + ts, :] = u_ref[0, :, dc:dc + ds]
    cext_ref[SC_HALO + ts:2 * SC_HALO + ts, :] = cn_ref[0] * keep_next

    pad_c = (kc - 1) // 2
    pad_s = (ks - 1) // 2
    for r in range(1, SUBLANES):
        ash_ref[r - 1] = aext_ref[r:r + ash_ref.shape[1], :]
    sc_shifts = _sc_shifts(ks)
    for slot, r in enumerate(sc_shifts):
        csh_ref[slot] = cext_ref[r:r + csh_ref.shape[1], :]

    groups = CONV_ROWS // SUBLANES

    def grouped(v):
        return v.reshape(groups, SUBLANES, v.shape[-1])

    def shifted(ext_ref, sh_ref, o, index):
        q, r = divmod(o, SUBLANES)
        rows = pl.ds(q * SUBLANES, CONV_ROWS)
        return grouped(ext_ref[rows, :] if r == 0 else sh_ref[index(r), rows, :])

    for c in range(ts // CONV_ROWS):
        r0 = c * CONV_ROWS
        conv = jnp.broadcast_to(sb_ref[...][None], (groups, SUBLANES, ds))
        for k in range(ks):
            tap = shifted(cext_ref, csh_ref, r0 + SC_HALO + k - pad_s, sc_shifts.index)
            conv = conv + sw_ref[k][None] * tap
        s = grouped(u_ref[0, r0:r0 + CONV_ROWS, dc + ds:dc + 2 * ds]) * conv
        ys_ref[r0:r0 + CONV_ROWS, :] = (
            _rms(s) * gos_ref[...][None]).reshape(CONV_ROWS, ds).astype(BF16)
    x1 = x_ref[0] + jnp.dot(ys_ref[...], wout_ref[dc:dc + ds, :], preferred_element_type=F32)

    for c in range(ts // CONV_ROWS):
        r0 = c * CONV_ROWS
        acc = jnp.broadcast_to(cb_ref[...][None], (groups, SUBLANES, dc))
        for k in range(kc):
            tap = shifted(aext_ref, ash_ref, r0 + CONF_HALO + k - pad_c, lambda r: r - 1)
            acc = acc + cw_ref[k][None] * tap
        mu = jnp.mean(acc, axis=-1, keepdims=True)
        cen = acc - mu
        var = jnp.mean(cen * cen, axis=-1, keepdims=True)
        a = cen * lax.rsqrt(var + EPS) * lg_ref[...][None] + lb_ref[...][None]
        a = a * _sigmoid(a)
        ya_ref[r0:r0 + CONV_ROWS, :] = (
            _rms(a) * goc_ref[...][None]).reshape(CONV_ROWS, dc).astype(BF16)
    x1 = x1 + jnp.dot(ya_ref[...], wout_ref[0:dc, :], preferred_element_type=F32)
    x1_ref[0] = x1
    h2 = (_rms(x1) * g2_ref[...]).astype(BF16)
    h2_ref[...] = h2
    logits = lax.dot_general(wrt_ref[...], h2, (((1,), (1,)), ((), ())),
                             preferred_element_type=F32)
    ex = jnp.exp(logits - jnp.max(logits, axis=0, keepdims=True))
    aff = ex / jnp.sum(ex, axis=0, keepdims=True)
    for j in range(ts // ROUTE_TILE):
        aff_ref[j] = aff[:, j * ROUTE_TILE:(j + 1) * ROUTE_TILE]


def _replicate_rows(v):
    return jnp.broadcast_to(v, (SUBLANES, v.shape[1]))


def _replicate_taps(w):
    return jnp.broadcast_to(w[:, None, :], (w.shape[0], SUBLANES, w.shape[1]))


def _mixer(u3, x3, cw, cb, lg, lb, sw, sb, goc, gos, wout, g2, wrt, ts, riders=()):
    b, s, d = x3.shape
    dc = cw.shape[1]
    ds = sw.shape[1]
    kc = cw.shape[0]
    ks = sw.shape[0]
    e = wrt.shape[0]
    nt = s // ts
    n = b * s
    hb_c = ts // CONF_HALO
    hb_s = ts // SC_HALO
    col_cv = dc // ds

    def const(shape):
        return pl.BlockSpec(shape, lambda i, t: (0,) * len(shape))

    in_specs = [
        pl.BlockSpec((1, ts, dc + 2 * ds), lambda i, t: (i, t, 0)),
        pl.BlockSpec((1, CONF_HALO, dc), lambda i, t: (i, jnp.maximum(t * hb_c - 1, 0), 0)),
        pl.BlockSpec((1, CONF_HALO, dc),
                     lambda i, t: (i, jnp.minimum((t + 1) * hb_c, s // CONF_HALO - 1), 0)),
        pl.BlockSpec((1, SC_HALO, ds),
                     lambda i, t: (i, jnp.maximum(t * hb_s - 1, 0), col_cv)),
        pl.BlockSpec((1, SC_HALO, ds),
                     lambda i, t: (i, jnp.minimum((t + 1) * hb_s, s // SC_HALO - 1), col_cv)),
        pl.BlockSpec((1, ts, d), lambda i, t: (i, t, 0)),
        const((kc, SUBLANES, dc)), const((SUBLANES, dc)), const((SUBLANES, dc)),
        const((SUBLANES, dc)),
        const((ks, SUBLANES, ds)), const((SUBLANES, ds)), const((SUBLANES, dc)),
        const((SUBLANES, ds)),
        pl.BlockSpec((dc + ds, d), lambda i, t: (0, 0), pipeline_mode=pl.Buffered(1)),
        const((1, d)), const((e, d)),
    ]
    out_specs = [
        pl.BlockSpec((1, ts, d), lambda i, t: (i, t, 0)),
        pl.BlockSpec((ts, d), lambda i, t: (i * nt + t, 0)),
        pl.BlockSpec((ts // ROUTE_TILE, e, ROUTE_TILE), lambda i, t: (i * nt + t, 0, 0)),
    ]
    out_shape = [
        jax.ShapeDtypeStruct((b, s, d), F32),
        jax.ShapeDtypeStruct((n, d), BF16),
        jax.ShapeDtypeStruct((n // ROUTE_TILE, e, ROUTE_TILE), F32),
    ]
    assert len(in_specs) == MIXER_INPUTS
    r_in, r_out, r_shapes = _rider_specs(riders, b * nt, lambda i, t: i * nt + t)
    outs = pl.pallas_call(
        functools.partial(_mixer_kernel, ts=ts, dc=dc, ds=ds, nt=nt, kc=kc, ks=ks,
                          nride=len(riders)),
        grid=(b, nt),
        in_specs=in_specs + r_in,
        out_specs=out_specs + r_out,
        out_shape=out_shape + r_shapes,
        scratch_shapes=[
            pltpu.VMEM((ts + 2 * CONF_HALO, dc), F32),
            pltpu.VMEM((ts + 2 * SC_HALO, ds), F32),
            pltpu.VMEM((SUBLANES - 1, ts + 2 * CONF_HALO - SUBLANES, dc), F32),
            pltpu.VMEM((len(_sc_shifts(ks)), ts + 2 * SC_HALO - SUBLANES, ds), F32),
            pltpu.VMEM((ts, dc), BF16),
            pltpu.VMEM((ts, ds), BF16),
        ],
        compiler_params=pltpu.CompilerParams(
            dimension_semantics=("parallel", "parallel"), vmem_limit_bytes=V7X_VMEM_LIMIT),
        name="mixer",
    )(u3, u3, u3, u3, u3, x3, _replicate_taps(cw), _replicate_rows(cb), _replicate_rows(lg),
      _replicate_rows(lb), _replicate_taps(sw), _replicate_rows(sb), _replicate_rows(goc),
      _replicate_rows(gos), wout, g2, wrt, *riders)
    return outs[:3], outs[3:]


def _route_kernel(aff_ref, rel_ref, gate_ref, base_ref, before_ref, *, cap, nc, e):
    capf = jnp.float32(cap)

    def count(pred):
        bits = pltpu.bitcast(aff_ref[...], I32)
        per_lane = jnp.sum(pred(bits).astype(F32), axis=0)
        return jnp.sum(per_lane, axis=-1, keepdims=True)

    def search(i, cur):
        cand = cur | lax.shift_left(jnp.int32(1), 30 - i)
        return jnp.where(count(lambda bits: bits >= cand[None]) >= capf, cand, cur)

    thr = lax.fori_loop(0, 31, search, jnp.zeros((e, 1), I32))
    need_eq = capf - count(lambda bits: bits > thr[None])

    row = lax.broadcasted_iota(I32, (ROUTE_TILE, ROUTE_TILE), 0)
    col = lax.broadcasted_iota(I32, (ROUTE_TILE, ROUTE_TILE), 1)
    tri = (row <= col).astype(BF16)

    def rank(flags):
        within = jnp.dot(flags.reshape(nc * e, ROUTE_TILE).astype(BF16), tri,
                         preferred_element_type=F32).reshape(nc, e, ROUTE_TILE)
        before_ref[...] = jnp.broadcast_to(within[:, :, ROUTE_TILE - 1:], (nc, e, 128))

        def tile_prefix(c, running):
            total = before_ref[c]
            before_ref[c] = running
            return running + total

        lax.fori_loop(0, nc, tile_prefix, jnp.zeros((e, 128), F32))
        before = before_ref[...][:, :, 0:1]
        return within - flags + before, before

    aff = aff_ref[...]
    bits = pltpu.bitcast(aff, I32)
    eq = bits == thr[None]
    eq_rank, _ = rank(eq.astype(F32))
    sel = (bits > thr[None]) | (eq & (eq_rank < need_eq[None]))
    slot, sel_before = rank(sel.astype(F32))
    window = jnp.floor(sel_before * (1.0 / SLOT_ALIGN)) * SLOT_ALIGN
    rel_ref[...] = jnp.where(sel, (slot - window).astype(I32), UNSELECTED)
    gate_ref[...] = jnp.where(sel, aff, 0.0)
    base_ref[...] = before_ref[...].astype(I32)


def _route(aff3, cap):
    nc, e, _ = aff3.shape
    full = lambda shape: pl.BlockSpec(shape, lambda: (0,) * len(shape))
    return pl.pallas_call(
        functools.partial(_route_kernel, cap=cap, nc=nc, e=e),
        in_specs=[full((nc, e, ROUTE_TILE))],
        out_specs=[full((nc, e, ROUTE_TILE)), full((nc, e, ROUTE_TILE)), full((nc, e, 128))],
        out_shape=[
            jax.ShapeDtypeStruct((nc, e, ROUTE_TILE), I32),
            jax.ShapeDtypeStruct((nc, e, ROUTE_TILE), F32),
            jax.ShapeDtypeStruct((nc, e, 128), I32),
        ],
        scratch_shapes=[pltpu.VMEM((nc, e, 128), F32)],
        compiler_params=pltpu.CompilerParams(vmem_limit_bytes=V7X_VMEM_LIMIT),
        name="route",
    )(aff3)


def _align_down(slot):
    shift = SLOT_ALIGN.bit_length() - 1
    return (slot >> shift) << shift


def _window_start(base_ref, i, ex, e):
    return _align_down(base_ref[i * e + ex])


def _dispatch_kernel(base_ref, npass_ref, h2_ref, rel_ref, gate_ref,
                     xe_ref, gs_ref,
                     stage_ref, gstage_ref, carry_ref, gcarry_ref, count_ref, sem,
                     *, e, cap, nc):
    m = SLOT_WINDOW
    i = pl.program_id(0)

    @pl.when(i == 0)
    def _():
        carry_ref[...] = jnp.zeros_like(carry_ref)
        gcarry_ref[...] = jnp.zeros_like(gcarry_ref)
        count_ref[0] = 0

    def copies(buf, ex, dst_row):
        rows = pl.ds(ex * m, m)
        return (
            pltpu.make_async_copy(stage_ref.at[buf, rows], xe_ref.at[ex, pl.ds(dst_row, m)],
                                  sem.at[0]),
            pltpu.make_async_copy(gstage_ref.at[buf, rows], gs_ref.at[ex, pl.ds(dst_row, m)],
                                  sem.at[1]),
        )

    def wait_all(buf):
        for ex in range(e):
            for cp in copies(buf, ex, 0):
                cp.wait()

    h2 = h2_ref[...]
    jota = lax.broadcasted_iota(I32, (e, m, ROUTE_TILE), 1)

    def one_pass(q, _):
        done = count_ref[0]
        buf = done % 2
        relq = rel_ref[0] - q * m
        hit = relq[:, None, :] == jota
        p = hit.astype(F32)
        stage_ref[buf] = jnp.dot(p.reshape(e * m, ROUTE_TILE).astype(BF16), h2,
                                 preferred_element_type=F32).astype(BF16)
        gsel = jnp.sum(p * gate_ref[0][:, None, :], axis=-1, keepdims=True)
        gstage_ref[buf] = jnp.broadcast_to(gsel, (e, m, 128)).reshape(e * m, 128)

        first = q == 0
        for ex in range(e):
            head = pl.ds(ex * m, SLOT_ALIGN)
            wp = _window_start(base_ref, i, ex, e)
            hi = base_ref[(i + 1) * e + ex]
            carry = carry_ref[ex]
            gcarry = gcarry_ref[ex]
            spent = jnp.zeros_like(carry)
            gspent = jnp.zeros_like(gcarry)
            stage_ref[buf, head, :] = stage_ref[buf, head, :] + jnp.where(first, carry, spent)
            gstage_ref[buf, head, :] = gstage_ref[buf, head, :] + jnp.where(first, gcarry, gspent)
            off = _align_down(hi) - wp - q * m
            due = (off >= 0) & (off < m)
            tail = pl.ds(pl.multiple_of(ex * m + jnp.where(due, off, 0), SLOT_ALIGN), SLOT_ALIGN)
            carry_ref[ex] = jnp.where(due, stage_ref[buf, tail, :],
                                      jnp.where(first, spent, carry))
            gcarry_ref[ex] = jnp.where(due, gstage_ref[buf, tail, :],
                                       jnp.where(first, gspent, gcarry))

        @pl.when(done > 0)
        def _():
            wait_all(1 - buf)

        for ex in range(e):
            wp = _window_start(base_ref, i, ex, e)
            hi = base_ref[(i + 1) * e + ex]
            needed = (hi - wp > q * m) | (q == 0)
            dst = jnp.where(needed, wp + q * m, cap)
            for cp in copies(buf, ex, pl.multiple_of(dst, SLOT_ALIGN)):
                cp.start()
        count_ref[0] = done + 1
        return 0

    lax.fori_loop(0, npass_ref[i], one_pass, 0)

    @pl.when(i == nc - 1)
    def _():
        wait_all((count_ref[0] - 1) % 2)
        stage_ref[0] = jnp.zeros(stage_ref.shape[1:], BF16)
        gstage_ref[0] = jnp.zeros(gstage_ref.shape[1:], F32)
        for ex in range(e):
            for cp in copies(0, ex, cap):
                cp.start()
        wait_all(0)


def _dispatch(base, npass, h2, rel3, gate3, cap):
    nc, e, _ = rel3.shape
    d = h2.shape[1]
    m = SLOT_WINDOW
    grid_spec = pltpu.PrefetchScalarGridSpec(
        num_scalar_prefetch=2,
        grid=(nc,),
        in_specs=[
            pl.BlockSpec((ROUTE_TILE, d), lambda i, *_: (i, 0)),
            pl.BlockSpec((1, e, ROUTE_TILE), lambda i, *_: (i, 0, 0)),
            pl.BlockSpec((1, e, ROUTE_TILE), lambda i, *_: (i, 0, 0)),
        ],
        out_specs=[pl.BlockSpec(memory_space=pl.ANY), pl.BlockSpec(memory_space=pl.ANY)],
        scratch_shapes=[
            pltpu.VMEM((2, e * m, d), BF16),
            pltpu.VMEM((2, e * m, 128), F32),
            pltpu.VMEM((e, SLOT_ALIGN, d), BF16),
            pltpu.VMEM((e, SLOT_ALIGN, 128), F32),
            pltpu.SMEM((1,), I32),
            pltpu.SemaphoreType.DMA((2,)),
        ],
    )
    return pl.pallas_call(
        functools.partial(_dispatch_kernel, e=e, cap=cap, nc=nc),
        grid_spec=grid_spec,
        out_shape=[
            jax.ShapeDtypeStruct((e, cap + m, d), BF16),
            jax.ShapeDtypeStruct((e, cap + m, 128), F32),
        ],
        compiler_params=pltpu.CompilerParams(
            dimension_semantics=("arbitrary",), vmem_limit_bytes=V7X_VMEM_LIMIT),
        name="dispatch",
    )(base, npass, h2, rel3, gate3)


def _ffn_kernel(xe_ref, gs_ref, wg_ref, wu_ref, wd_ref, y_ref, hid_ref, *, nf, fc):
    p = pl.program_id(2)

    @pl.when(p < nf)
    def _():
        xe = xe_ref[0]
        g = jnp.dot(xe, wg_ref[0], preferred_element_type=F32)
        u = jnp.dot(xe, wu_ref[0], preferred_element_type=F32)
        hid_ref[p] = (g * _sigmoid(g) * u).astype(BF16)

    @pl.when(p >= nf)
    def _():
        y = jnp.dot(hid_ref[0], wd_ref[0, 0:fc, :], preferred_element_type=F32)
        for f in range(1, nf):
            y = y + jnp.dot(hid_ref[f], wd_ref[0, f * fc:(f + 1) * fc, :],
                            preferred_element_type=F32)
        y_ref[0] = (y * gs_ref[0][:, 0:1]).astype(BF16)


def _ffn(xe, gs, wg, wu, wd, cap, rt, fc, oc):
    e, _, d = xe.shape
    dff = wg.shape[2]
    nf = dff // fc
    no = d // oc
    out_col = lambda p: jnp.maximum(p - nf, 0)
    return pl.pallas_call(
        functools.partial(_ffn_kernel, nf=nf, fc=fc),
        grid=(e, cap // rt, nf + no),
        in_specs=[
            pl.BlockSpec((1, rt, d), lambda ex, r, p: (ex, r, 0)),
            pl.BlockSpec((1, rt, 128), lambda ex, r, p: (ex, r, 0)),
            pl.BlockSpec((1, d, fc), lambda ex, r, p: (ex, 0, jnp.minimum(p, nf - 1))),
            pl.BlockSpec((1, d, fc), lambda ex, r, p: (ex, 0, jnp.minimum(p, nf - 1))),
            pl.BlockSpec((1, dff, oc), lambda ex, r, p: (ex, 0, out_col(p))),
        ],
        out_specs=pl.BlockSpec((1, rt, oc), lambda ex, r, p: (ex, r, out_col(p))),
        out_shape=jax.ShapeDtypeStruct((e, cap, d), BF16),
        scratch_shapes=[pltpu.VMEM((nf, rt, fc), BF16)],
        compiler_params=pltpu.CompilerParams(
            dimension_semantics=("parallel", "parallel", "arbitrary"),
            vmem_limit_bytes=V7X_VMEM_LIMIT),
        name="ffn",
    )(xe, gs, wg, wu, wd)


def _combine_kernel(base_ref, npass_ref, x1_ref, rel_ref, gf_ref, y_ref, out_ref,
                    ystage_ref, sem, *, e, cap, nc):
    m = SLOT_WINDOW
    i = pl.program_id(0)
    buf = i % 2
    jota = lax.broadcasted_iota(I32, (e, m, ROUTE_TILE), 1)
    eota = lax.broadcasted_iota(I32, (e, 1, 1), 0)

    def windows(tile, q, into):
        shift = jnp.zeros((e, 1, 1), I32)
        cps = []
        for ex in range(e):
            wp = _window_start(base_ref, tile, ex, e) + q * m
            rp = jnp.minimum(wp, cap - m)
            shift = jnp.where(eota == ex, wp - rp, shift)
            cps.append(pltpu.make_async_copy(
                y_ref.at[ex, pl.ds(pl.multiple_of(rp, SLOT_ALIGN), m)],
                ystage_ref.at[into, pl.ds(ex * m, m)], sem.at[into]))
        return cps, shift

    def expand(q, shift, acc):
        relq = (rel_ref[0] - q * m)[:, None, :]
        hit = (relq >= 0) & (relq < m) & (relq + shift == jota)
        p = hit.astype(F32).reshape(e * m, ROUTE_TILE).astype(BF16)
        return acc + lax.dot_general(p, ystage_ref[buf], (((0,), (0,)), ((), ())),
                                     preferred_element_type=F32)

    @pl.when(i == 0)
    def _():
        for cp in windows(0, 0, 0)[0]:
            cp.start()

    @pl.when(i + 1 < nc)
    def _():
        for cp in windows(i + 1, 0, 1 - buf)[0]:
            cp.start()

    cps, shift = windows(i, 0, buf)
    for cp in cps:
        cp.wait()
    acc = expand(0, shift, x1_ref[...])

    def extra_pass(q, acc):
        cps, shift = windows(i, q, buf)
        for cp in cps:
            cp.start()
        for cp in cps:
            cp.wait()
        return expand(q, shift, acc)

    x2 = lax.fori_loop(1, npass_ref[i], extra_pass, acc)
    out_ref[...] = _rms(x2) * gf_ref[...]


def _combine(base, npass, x1, rel3, gf, y, cap):
    nc, e, _ = rel3.shape
    n, d = x1.shape
    m = SLOT_WINDOW
    grid_spec = pltpu.PrefetchScalarGridSpec(
        num_scalar_prefetch=2,
        grid=(nc,),
        in_specs=[
            pl.BlockSpec((ROUTE_TILE, d), lambda i, *_: (i, 0)),
            pl.BlockSpec((1, e, ROUTE_TILE), lambda i, *_: (i, 0, 0)),
            pl.BlockSpec((1, d), lambda i, *_: (0, 0)),
            pl.BlockSpec(memory_space=pl.ANY),
        ],
        out_specs=pl.BlockSpec((ROUTE_TILE, d), lambda i, *_: (i, 0)),
        scratch_shapes=[
            pltpu.VMEM((2, e * m, d), BF16),
            pltpu.SemaphoreType.DMA((2,)),
        ],
    )
    return pl.pallas_call(
        functools.partial(_combine_kernel, e=e, cap=cap, nc=nc),
        grid_spec=grid_spec,
        out_shape=jax.ShapeDtypeStruct((n, d), F32),
        compiler_params=pltpu.CompilerParams(
            dimension_semantics=("arbitrary",), vmem_limit_bytes=V7X_VMEM_LIMIT),
        name="combine",
    )(base, npass, x1, rel3, gf, y)


def _largest_tile(total, limit, multiple):
    t = min(total, limit)
    while total % t or t % multiple:
        t -= 1
    return t


def _trunk(x, w, cap_factor, experts):
    b, s, d = x.shape
    n = b * s
    e = w["wrt"].shape[0]
    dc = w["cw"].shape[1]
    ds = w["sw"].shape[1]
    cap = cap_factor * n // e
    assert n % ROUTE_TILE == 0 and s % ROUTE_TILE == 0 and cap >= SLOT_WINDOW
    assert cap % SLOT_ALIGN == 0

    tm = _largest_tile(n, 256, 8)
    ts = _largest_tile(s, 256, ROUTE_TILE)
    flat = lambda v: v.reshape(-1, v.shape[-1])
    ride_in = [] if experts else [flat(w["wg32"]), flat(w["wu32"])]
    ride_mix = [] if experts else [flat(w["wd32"])]
    u, cast_in = _inproj(x.reshape(n, d), w["g1"], w["w_in"], dc, ds, tm, ride_in)
    (x1, h2, aff3), cast_mix = _mixer(
        u.reshape(b, s, dc + 2 * ds), x, w["cw"], w["cb"], w["lg"], w["lb"], w["sw"], w["sb"],
        w["goc"], w["gos"], w["w_out"], w["g2"], w["wrt"], ts, ride_mix)
    if not experts:
        experts = tuple(v.reshape(w["wg32"].shape[0], -1, v.shape[-1])
                        for v in (*cast_in, *cast_mix))
    wg, wu, wd = experts
    rel3, gate3, base3 = _route(aff3, cap)

    base = jnp.concatenate([base3[:, :, 0], jnp.full((1, e), cap, I32)], axis=0)
    rows = base[1:] - _align_down(base[:-1])
    npass = jnp.maximum((jnp.max(rows, axis=1) + SLOT_WINDOW - 1) // SLOT_WINDOW, 1).astype(I32)
    base = base.reshape(-1)

    xe, gs = _dispatch(base, npass, h2, rel3, gate3, cap)
    rt = _largest_tile(cap, 1024, SLOT_ALIGN)
    fc = _largest_tile(wg.shape[2], 512, 128)
    oc = _largest_tile(d, 1024, 128)
    y = _ffn(xe, gs, wg, wu, wd, cap, rt, fc, oc)
    out = _combine(base, npass, x1.reshape(n, d), rel3, w["gf"], y, cap)
    return out.reshape(b, s, d), experts


def kernel(x_prompt, x_sample, g_norm1, w_in, conf_dw_w, conf_dw_b, conf_ln_g, conf_ln_b,
           sc_dw_w, sc_dw_b, g_out_conf, g_out_sc, w_out, g_norm2, w_router, w_gate, w_up,
           w_down, g_final):
    depth = w_in.shape[0]
    cap_factor = 2
    row = lambda v: v.reshape(1, -1)
    layers = []
    for l in range(depth):
        layers.append(dict(
            g1=row(g_norm1[l]), w_in=w_in[l].astype(BF16),
            cw=conf_dw_w[l], cb=row(conf_dw_b[l]), lg=row(conf_ln_g[l]), lb=row(conf_ln_b[l]),
            sw=sc_dw_w[l], sb=row(sc_dw_b[l]), goc=row(g_out_conf[l]), gos=row(g_out_sc[l]),
            w_out=w_out[l].astype(BF16), g2=row(g_norm2[l]),
            wrt=w_router[l].T.astype(BF16),
            wg32=w_gate[l], wu32=w_up[l], wd32=w_down[l],
        ))
    assert depth == 1, "the final norm is fused into the last layer's combine"
    outs = []
    experts = None
    for x in (x_prompt, x_sample):
        w = dict(layers[0], gf=row(g_final))
        out, experts = _trunk(x, w, cap_factor, experts)
        outs.append(out)
    return tuple(outs)
```

```python
import functools

import jax
import jax.numpy as jnp
from jax import lax
from jax.experimental import pallas as pl
from jax.experimental.pallas import tpu as pltpu

F32 = jnp.float32
BF16 = jnp.bfloat16
I32 = jnp.int32

EPS = 1e-6
CONF_HALO = 16
SC_HALO = 8
SUBLANES = 8
CONV_ROWS = 32
ROUTE_TILE = 256
SLOT_WINDOW = 64
SLOT_ALIGN = 16
COMBINE_AHEAD = 2
UNSELECTED = -(1 << 20)
V7X_VMEM_LIMIT = 56 * 1024 * 1024


def _sigmoid(x):
    return 1.0 / (1.0 + jnp.exp(-x))


def _rms(x):
    return x * lax.rsqrt(jnp.mean(x * x, axis=-1, keepdims=True) + EPS)


def _rider_specs(riders, steps, step_of):
    in_specs, out_specs, out_shapes = [], [], []
    for w in riders:
        rows, cols = w.shape
        assert rows % (steps * SLOT_ALIGN) == 0, (rows, steps)
        spec = pl.BlockSpec((rows // steps, cols), lambda *g: (step_of(*g), 0))
        in_specs.append(spec)
        out_specs.append(spec)
        out_shapes.append(jax.ShapeDtypeStruct(w.shape, BF16))
    return in_specs, out_specs, out_shapes


def _cast_riders(srcs, dsts):
    for src, dst in zip(srcs, dsts):
        dst[...] = src[...].astype(BF16)


def _inproj_kernel(*refs, dc, ds, nride):
    x_ref, g_ref, w_ref = refs[:3]
    u_ref = refs[3 + nride]
    _cast_riders(refs[3:3 + nride], refs[4 + nride:])
    h = (_rms(x_ref[...]) * g_ref[...]).astype(BF16)

    def proj(c0, width):
        return jnp.dot(h, w_ref[:, c0:c0 + width], preferred_element_type=F32)

    u_ref[:, 0:dc] = proj(0, dc) * _sigmoid(proj(dc, dc))
    cg = proj(2 * dc + ds, ds)
    u_ref[:, dc:dc + ds] = cg * proj(2 * dc + 2 * ds, ds)
    u_ref[:, dc + ds:dc + 2 * ds] = proj(2 * dc, ds)


def _inproj(x2, g1, w_in, dc, ds, tm, riders=()):
    n, d = x2.shape
    cols = w_in.shape[1]
    r_in, r_out, r_shapes = _rider_specs(riders, n // tm, lambda i: i)
    outs = pl.pallas_call(
        functools.partial(_inproj_kernel, dc=dc, ds=ds, nride=len(riders)),
        grid=(n // tm,),
        in_specs=[
            pl.BlockSpec((tm, d), lambda i: (i, 0)),
            pl.BlockSpec((1, d), lambda i: (0, 0)),
            pl.BlockSpec((d, cols), lambda i: (0, 0), pipeline_mode=pl.Buffered(1)),
        ] + r_in,
        out_specs=[pl.BlockSpec((tm, dc + 2 * ds), lambda i: (i, 0))] + r_out,
        out_shape=[jax.ShapeDtypeStruct((n, dc + 2 * ds), F32)] + r_shapes,
        compiler_params=pltpu.CompilerParams(
            dimension_semantics=("parallel",), vmem_limit_bytes=V7X_VMEM_LIMIT),
        name="inproj",
    )(x2, g1, w_in, *riders)
    return outs[0], outs[1:]


def _sc_shifts(ks):
    pad = (ks - 1) // 2
    return sorted({(SC_HALO + k - pad) % SUBLANES for k in range(ks)} - {0})


MIXER_INPUTS = 17


def _mixer_kernel(*refs, ts, dc, ds, nt, kc, ks, nride):
    (u_ref, ap_ref, an_ref, cp_ref, cn_ref, x_ref,
     cw_ref, cb_ref, lg_ref, lb_ref, sw_ref, sb_ref, goc_ref, gos_ref,
     wout_ref, g2_ref, wrt_ref) = refs[:MIXER_INPUTS]
    outs = refs[MIXER_INPUTS + nride:]
    x1_ref, h2_ref, aff_ref = outs[:3]
    aext_ref, cext_ref, ash_ref, csh_ref, ya_ref, ys_ref = outs[3 + nride:]
    _cast_riders(refs[MIXER_INPUTS:MIXER_INPUTS + nride], outs[3:3 + nride])
    t = pl.program_id(1)
    keep_prev = (t > 0).astype(F32)
    keep_next = (t < nt - 1).astype(F32)
    aext_ref[0:CONF_HALO, :] = ap_ref[0] * keep_prev
    aext_ref[CONF_HALO:CONF_HALO + ts, :] = u_ref[0, :, 0:dc]
    aext_ref[CONF_HALO + ts:2 * CONF_HALO + ts, :] = an_ref[0] * keep_next
    cext_ref[0:SC_HALO, :] = cp_ref[0] * keep_prev
    cext_ref[SC_HALO:SC_HALO + ts, :] = u_ref[0, :, dc:dc + ds]
    cext_ref[SC_HALO + ts:2 * SC_HALO + ts, :] = cn_ref[0] * keep_next

    pad_c = (kc - 1) // 2
    pad_s = (ks - 1) // 2
    for r in range(1, SUBLANES):
        ash_ref[r - 1] = aext_ref[r:r + ash_ref.shape[1], :]
    sc_shifts = _sc_shifts(ks)
    for slot, r in enumerate(sc_shifts):
        csh_ref[slot] = cext_ref[r:r + csh_ref.shape[1], :]

    groups = CONV_ROWS // SUBLANES

    def grouped(v):
        return v.reshape(groups, SUBLANES, v.shape[-1])

    def shifted(ext_ref, sh_ref, o, index):
        q, r = divmod(o, SUBLANES)
        rows = pl.ds(q * SUBLANES, CONV_ROWS)
        return grouped(ext_ref[rows, :] if r == 0 else sh_ref[index(r), rows, :])

    for c in range(ts // CONV_ROWS):
        r0 = c * CONV_ROWS
        conv = jnp.broadcast_to(sb_ref[...][None], (groups, SUBLANES, ds))
        for k in range(ks):
            tap = shifted(cext_ref, csh_ref, r0 + SC_HALO + k - pad_s, sc_shifts.index)
            conv = conv + sw_ref[k][None] * tap
        s = grouped(u_ref[0, r0:r0 + CONV_ROWS, dc + ds:dc + 2 * ds]) * conv
        ys_ref[r0:r0 + CONV_ROWS, :] = (
            _rms(s) * gos_ref[...][None]).reshape(CONV_ROWS, ds).astype(BF16)
    x1 = x_ref[0] + jnp.dot(ys_ref[...], wout_ref[dc:dc + ds, :], preferred_element_type=F32)

    for c in range(ts // CONV_ROWS):
        r0 = c * CONV_ROWS
        acc = jnp.broadcast_to(cb_ref[...][None], (groups, SUBLANES, dc))
        for k in range(kc):
            tap = shifted(aext_ref, ash_ref, r0 + CONF_HALO + k - pad_c, lambda r: r - 1)
            acc = acc + cw_ref[k][None] * tap
        mu = jnp.mean(acc, axis=-1, keepdims=True)
        cen = acc - mu
        var = jnp.mean(cen * cen, axis=-1, keepdims=True)
        a = cen * lax.rsqrt(var + EPS) * lg_ref[...][None] + lb_ref[...][None]
        a = a * _sigmoid(a)
        ya_ref[r0:r0 + CONV_ROWS, :] = (
            _rms(a) * goc_ref[...][None]).reshape(CONV_ROWS, dc).astype(BF16)
    x1 = x1 + jnp.dot(ya_ref[...], wout_ref[0:dc, :], preferred_element_type=F32)
    x1_ref[0] = x1
    h2 = (_rms(x1) * g2_ref[...]).astype(BF16)
    h2_ref[...] = h2
    logits = lax.dot_general(wrt_ref[...], h2, (((1,), (1,)), ((), ())),
                             preferred_element_type=F32)
    ex = jnp.exp(logits - jnp.max(logits, axis=0, keepdims=True))
    aff = ex / jnp.sum(ex, axis=0, keepdims=True)
    for j in range(ts // ROUTE_TILE):
        aff_ref[j] = aff[:, j * ROUTE_TILE:(j + 1) * ROUTE_TILE]


def _replicate_rows(v):
    return jnp.broadcast_to(v, (SUBLANES, v.shape[1]))


def _replicate_taps(w):
    return jnp.broadcast_to(w[:, None, :], (w.shape[0], SUBLANES, w.shape[1]))


def _mixer(u3, x3, cw, cb, lg, lb, sw, sb, goc, gos, wout, g2, wrt, ts, riders=()):
    b, s, d = x3.shape
    dc = cw.shape[1]
    ds = sw.shape[1]
    kc = cw.shape[0]
    ks = sw.shape[0]
    e = wrt.shape[0]
    nt = s // ts
    n = b * s
    hb_c = ts // CONF_HALO
    hb_s = ts // SC_HALO
    col_cv = dc // ds

    def const(shape):
        return pl.BlockSpec(shape, lambda i, t: (0,) * len(shape))

    in_specs = [
        pl.BlockSpec((1, ts, dc + 2 * ds), lambda i, t: (i, t, 0)),
        pl.BlockSpec((1, CONF_HALO, dc), lambda i, t: (i, jnp.maximum(t * hb_c - 1, 0), 0)),
        pl.BlockSpec((1, CONF_HALO, dc),
                     lambda i, t: (i, jnp.minimum((t + 1) * hb_c, s // CONF_HALO - 1), 0)),
        pl.BlockSpec((1, SC_HALO, ds),
                     lambda i, t: (i, jnp.maximum(t * hb_s - 1, 0), col_cv)),
        pl.BlockSpec((1, SC_HALO, ds),
                     lambda i, t: (i, jnp.minimum((t + 1) * hb_s, s // SC_HALO - 1), col_cv)),
        pl.BlockSpec((1, ts, d), lambda i, t: (i, t, 0)),
        const((kc, SUBLANES, dc)), const((SUBLANES, dc)), const((SUBLANES, dc)),
        const((SUBLANES, dc)),
        const((ks, SUBLANES, ds)), const((SUBLANES, ds)), const((SUBLANES, dc)),
        const((SUBLANES, ds)),
        pl.BlockSpec((dc + ds, d), lambda i, t: (0, 0), pipeline_mode=pl.Buffered(1)),
        const((1, d)), const((e, d)),
    ]
    out_specs = [
        pl.BlockSpec((1, ts, d), lambda i, t: (i, t, 0)),
        pl.BlockSpec((ts, d), lambda i, t: (i * nt + t, 0)),
        pl.BlockSpec((ts // ROUTE_TILE, e, ROUTE_TILE), lambda i, t: (i * nt + t, 0, 0)),
    ]
    out_shape = [
        jax.ShapeDtypeStruct((b, s, d), F32),
        jax.ShapeDtypeStruct((n, d), BF16),
        jax.ShapeDtypeStruct((n // ROUTE_TILE, e, ROUTE_TILE), F32),
    ]
    assert len(in_specs) == MIXER_INPUTS
    r_in, r_out, r_shapes = _rider_specs(riders, b * nt, lambda i, t: i * nt + t)
    outs = pl.pallas_call(
        functools.partial(_mixer_kernel, ts=ts, dc=dc, ds=ds, nt=nt, kc=kc, ks=ks,
                          nride=len(riders)),
        grid=(b, nt),
        in_specs=in_specs + r_in,
        out_specs=out_specs + r_out,
        out_shape=out_shape + r_shapes,
        scratch_shapes=[
            pltpu.VMEM((ts + 2 * CONF_HALO, dc), F32),
            pltpu.VMEM((ts + 2 * SC_HALO, ds), F32),
            pltpu.VMEM((SUBLANES - 1, ts + 2 * CONF_HALO - SUBLANES, dc), F32),
            pltpu.VMEM((len(_sc_shifts(ks)), ts + 2 * SC_HALO - SUBLANES, ds), F32),
            pltpu.VMEM((ts, dc), BF16),
            pltpu.VMEM((ts, ds), BF16),
        ],
        compiler_params=pltpu.CompilerParams(
            dimension_semantics=("parallel", "parallel"), vmem_limit_bytes=V7X_VMEM_LIMIT),
        name="mixer",
    )(u3, u3, u3, u3, u3, x3, _replicate_taps(cw), _replicate_rows(cb), _replicate_rows(lg),
      _replicate_rows(lb), _replicate_taps(sw), _replicate_rows(sb), _replicate_rows(goc),
      _replicate_rows(gos), wout, g2, wrt, *riders)
    return outs[:3], outs[3:]


def _route_kernel(aff_ref, rel_ref, gate_ref, base_ref, before_ref, *, cap, nc, e):
    capf = jnp.float32(cap)

    def count(pred):
        bits = pltpu.bitcast(aff_ref[...], I32)
        per_lane = jnp.sum(pred(bits).astype(F32), axis=0)
        return jnp.sum(per_lane, axis=-1, keepdims=True)

    def search(i, cur):
        cand = cur | lax.shift_left(jnp.int32(1), 30 - i)
        return jnp.where(count(lambda bits: bits >= cand[None]) >= capf, cand, cur)

    thr = lax.fori_loop(0, 31, search, jnp.zeros((e, 1), I32))
    need_eq = capf - count(lambda bits: bits > thr[None])

    row = lax.broadcasted_iota(I32, (ROUTE_TILE, ROUTE_TILE), 0)
    col = lax.broadcasted_iota(I32, (ROUTE_TILE, ROUTE_TILE), 1)
    tri = (row <= col).astype(BF16)

    def rank(flags):
        within = jnp.dot(flags.reshape(nc * e, ROUTE_TILE).astype(BF16), tri,
                         preferred_element_type=F32).reshape(nc, e, ROUTE_TILE)
        before_ref[...] = jnp.broadcast_to(within[:, :, ROUTE_TILE - 1:], (nc, e, 128))

        def tile_prefix(c, running):
            total = before_ref[c]
            before_ref[c] = running
            return running + total

        lax.fori_loop(0, nc, tile_prefix, jnp.zeros((e, 128), F32))
        before = before_ref[...][:, :, 0:1]
        return within - flags + before, before

    aff = aff_ref[...]
    bits = pltpu.bitcast(aff, I32)
    eq = bits == thr[None]
    eq_rank, _ = rank(eq.astype(F32))
    sel = (bits > thr[None]) | (eq & (eq_rank < need_eq[None]))
    slot, sel_before = rank(sel.astype(F32))
    window = jnp.floor(sel_before * (1.0 / SLOT_ALIGN)) * SLOT_ALIGN
    rel_ref[...] = jnp.where(sel, (slot - window).astype(I32), UNSELECTED)
    gate_ref[...] = jnp.where(sel, aff, 0.0)
    base_ref[...] = before_ref[...].astype(I32)


def _route(aff3, cap):
    nc, e, _ = aff3.shape
    full = lambda shape: pl.BlockSpec(shape, lambda: (0,) * len(shape))
    return pl.pallas_call(
        functools.partial(_route_kernel, cap=cap, nc=nc, e=e),
        in_specs=[full((nc, e, ROUTE_TILE))],
        out_specs=[full((nc, e, ROUTE_TILE)), full((nc, e, ROUTE_TILE)), full((nc, e, 128))],
        out_shape=[
            jax.ShapeDtypeStruct((nc, e, ROUTE_TILE), I32),
            jax.ShapeDtypeStruct((nc, e, ROUTE_TILE), F32),
            jax.ShapeDtypeStruct((nc, e, 128), I32),
        ],
        scratch_shapes=[pltpu.VMEM((nc, e, 128), F32)],
        compiler_params=pltpu.CompilerParams(vmem_limit_bytes=V7X_VMEM_LIMIT),
        name="route",
    )(aff3)


def _align_down(slot):
    shift = SLOT_ALIGN.bit_length() - 1
    return (slot >> shift) << shift


def _window_start(base_ref, i, ex, e):
    return _align_down(base_ref[i * e + ex])


def _dispatch_kernel(base_ref, npass_ref, h2_ref, rel_ref, gate_ref,
                     xe_ref, gs_ref,
                     stage_ref, gstage_ref, carry_ref, gcarry_ref, count_ref, sem,
                     *, e, cap, nc):
    m = SLOT_WINDOW
    i = pl.program_id(0)

    @pl.when(i == 0)
    def _():
        carry_ref[...] = jnp.zeros_like(carry_ref)
        gcarry_ref[...] = jnp.zeros_like(gcarry_ref)
        count_ref[0] = 0

    def copies(buf, ex, dst_row):
        rows = pl.ds(ex * m, m)
        return (
            pltpu.make_async_copy(stage_ref.at[buf, rows], xe_ref.at[ex, pl.ds(dst_row, m)],
                                  sem.at[0]),
            pltpu.make_async_copy(gstage_ref.at[buf, rows], gs_ref.at[ex, pl.ds(dst_row, m)],
                                  sem.at[1]),
        )

    def wait_all(buf):
        for ex in range(e):
            for cp in copies(buf, ex, 0):
                cp.wait()

    h2 = h2_ref[...]
    jota = lax.broadcasted_iota(I32, (e, m, ROUTE_TILE), 1)

    def one_pass(q, _):
        done = count_ref[0]
        buf = done % 2
        relq = rel_ref[0] - q * m
        hit = relq[:, None, :] == jota
        p = hit.astype(F32)
        stage_ref[buf] = jnp.dot(p.reshape(e * m, ROUTE_TILE).astype(BF16), h2,
                                 preferred_element_type=F32).astype(BF16)
        gsel = jnp.sum(p * gate_ref[0][:, None, :], axis=-1, keepdims=True)
        gstage_ref[buf] = jnp.broadcast_to(gsel, (e, m, 128)).reshape(e * m, 128)

        for ex in range(e):
            head = pl.ds(ex * m, SLOT_ALIGN)
            wp = _window_start(base_ref, i, ex, e)
            hi = base_ref[(i + 1) * e + ex]

            @pl.when(q == 0)
            def _():
                stage_ref[buf, head, :] = stage_ref[buf, head, :] + carry_ref[ex]
                gstage_ref[buf, head, :] = gstage_ref[buf, head, :] + gcarry_ref[ex]
                carry_ref[ex] = jnp.zeros((SLOT_ALIGN, carry_ref.shape[-1]), BF16)
                gcarry_ref[ex] = jnp.zeros((SLOT_ALIGN, 128), F32)

            off = _align_down(hi) - wp - q * m

            @pl.when((off >= 0) & (off < m))
            def _():
                tail = pl.ds(pl.multiple_of(ex * m + off, SLOT_ALIGN), SLOT_ALIGN)
                carry_ref[ex] = stage_ref[buf, tail, :]
                gcarry_ref[ex] = gstage_ref[buf, tail, :]

        @pl.when(done > 0)
        def _():
            wait_all(1 - buf)

        for ex in range(e):
            wp = _window_start(base_ref, i, ex, e)
            hi = base_ref[(i + 1) * e + ex]
            needed = (hi - wp > q * m) | (q == 0)
            dst = jnp.where(needed, wp + q * m, cap)
            for cp in copies(buf, ex, pl.multiple_of(dst, SLOT_ALIGN)):
                cp.start()
        count_ref[0] = done + 1
        return 0

    lax.fori_loop(0, npass_ref[i], one_pass, 0)

    @pl.when(i == nc - 1)
    def _():
        wait_all((count_ref[0] - 1) % 2)
        stage_ref[0] = jnp.zeros(stage_ref.shape[1:], BF16)
        gstage_ref[0] = jnp.zeros(gstage_ref.shape[1:], F32)
        for ex in range(e):
            for cp in copies(0, ex, cap):
                cp.start()
        wait_all(0)


def _dispatch(base, npass, h2, rel3, gate3, cap):
    nc, e, _ = rel3.shape
    d = h2.shape[1]
    m = SLOT_WINDOW
    grid_spec = pltpu.PrefetchScalarGridSpec(
        num_scalar_prefetch=2,
        grid=(nc,),
        in_specs=[
            pl.BlockSpec((ROUTE_TILE, d), lambda i, *_: (i, 0)),
            pl.BlockSpec((1, e, ROUTE_TILE), lambda i, *_: (i, 0, 0)),
            pl.BlockSpec((1, e, ROUTE_TILE), lambda i, *_: (i, 0, 0)),
        ],
        out_specs=[pl.BlockSpec(memory_space=pl.ANY), pl.BlockSpec(memory_space=pl.ANY)],
        scratch_shapes=[
            pltpu.VMEM((2, e * m, d), BF16),
            pltpu.VMEM((2, e * m, 128), F32),
            pltpu.VMEM((e, SLOT_ALIGN, d), BF16),
            pltpu.VMEM((e, SLOT_ALIGN, 128), F32),
            pltpu.SMEM((1,), I32),
            pltpu.SemaphoreType.DMA((2,)),
        ],
    )
    return pl.pallas_call(
        functools.partial(_dispatch_kernel, e=e, cap=cap, nc=nc),
        grid_spec=grid_spec,
        out_shape=[
            jax.ShapeDtypeStruct((e, cap + m, d), BF16),
            jax.ShapeDtypeStruct((e, cap + m, 128), F32),
        ],
        compiler_params=pltpu.CompilerParams(
            dimension_semantics=("arbitrary",), vmem_limit_bytes=V7X_VMEM_LIMIT),
        name="dispatch",
    )(base, npass, h2, rel3, gate3)


def _ffn_kernel(xe_ref, gs_ref, wg_ref, wu_ref, wd_ref, y_ref, hid_ref, *, nf, fc):
    p = pl.program_id(2)

    @pl.when(p < nf)
    def _():
        xe = xe_ref[0]
        g = jnp.dot(xe, wg_ref[0], preferred_element_type=F32)
        u = jnp.dot(xe, wu_ref[0], preferred_element_type=F32)
        hid_ref[p] = (g * _sigmoid(g) * u).astype(BF16)

    @pl.when(p >= nf)
    def _():
        y = jnp.dot(hid_ref[0], wd_ref[0, 0:fc, :], preferred_element_type=F32)
        for f in range(1, nf):
            y = y + jnp.dot(hid_ref[f], wd_ref[0, f * fc:(f + 1) * fc, :],
                            preferred_element_type=F32)
        y_ref[0] = (y * gs_ref[0][:, 0:1]).astype(BF16)


def _ffn(xe, gs, wg, wu, wd, cap, rt, fc, oc):
    e, _, d = xe.shape
    dff = wg.shape[2]
    nf = dff // fc
    no = d // oc
    out_col = lambda p: jnp.maximum(p - nf, 0)
    return pl.pallas_call(
        functools.partial(_ffn_kernel, nf=nf, fc=fc),
        grid=(e, cap // rt, nf + no),
        in_specs=[
            pl.BlockSpec((1, rt, d), lambda ex, r, p: (ex, r, 0)),
            pl.BlockSpec((1, rt, 128), lambda ex, r, p: (ex, r, 0)),
            pl.BlockSpec((1, d, fc), lambda ex, r, p: (ex, 0, jnp.minimum(p, nf - 1))),
            pl.BlockSpec((1, d, fc), lambda ex, r, p: (ex, 0, jnp.minimum(p, nf - 1))),
            pl.BlockSpec((1, dff, oc), lambda ex, r, p: (ex, 0, out_col(p))),
        ],
        out_specs=pl.BlockSpec((1, rt, oc), lambda ex, r, p: (ex, r, out_col(p))),
        out_shape=jax.ShapeDtypeStruct((e, cap, d), BF16),
        scratch_shapes=[pltpu.VMEM((nf, rt, fc), BF16)],
        compiler_params=pltpu.CompilerParams(
            dimension_semantics=("parallel", "parallel", "arbitrary"),
            vmem_limit_bytes=V7X_VMEM_LIMIT),
        name="ffn",
    )(xe, gs, wg, wu, wd)


def _combine_kernel(base_ref, npass_ref, x1_ref, rel_ref, gf_ref, y_ref, out_ref,
                    ystage_ref, sem, *, e, cap, nc):
    m = SLOT_WINDOW
    i = pl.program_id(0)
    ring = COMBINE_AHEAD + 1
    buf = i % ring
    jota = lax.broadcasted_iota(I32, (e, m, ROUTE_TILE), 1)
    eota = lax.broadcasted_iota(I32, (e, 1, 1), 0)

    def windows(tile, q, into):
        shift = jnp.zeros((e, 1, 1), I32)
        cps = []
        for ex in range(e):
            wp = _window_start(base_ref, tile, ex, e) + q * m
            rp = jnp.minimum(wp, cap - m)
            shift = jnp.where(eota == ex, wp - rp, shift)
            cps.append(pltpu.make_async_copy(
                y_ref.at[ex, pl.ds(pl.multiple_of(rp, SLOT_ALIGN), m)],
                ystage_ref.at[into, pl.ds(ex * m, m)], sem.at[into]))
        return cps, shift

    def expand(q, shift, acc):
        relq = (rel_ref[0] - q * m)[:, None, :]
        hit = (relq >= 0) & (relq < m) & (relq + shift == jota)
        p = hit.astype(F32).reshape(e * m, ROUTE_TILE).astype(BF16)
        return acc + lax.dot_general(p, ystage_ref[buf], (((0,), (0,)), ((), ())),
                                     preferred_element_type=F32)

    @pl.when(i == 0)
    def _():
        for tile in range(min(COMBINE_AHEAD, nc)):
            for cp in windows(tile, 0, tile)[0]:
                cp.start()

    @pl.when(i + COMBINE_AHEAD < nc)
    def _():
        for cp in windows(i + COMBINE_AHEAD, 0, (i + COMBINE_AHEAD) % ring)[0]:
            cp.start()

    cps, shift = windows(i, 0, buf)
    for cp in cps:
        cp.wait()
    acc = expand(0, shift, x1_ref[...])

    def extra_pass(q, acc):
        cps, shift = windows(i, q, buf)
        for cp in cps:
            cp.start()
        for cp in cps:
            cp.wait()
        return expand(q, shift, acc)

    x2 = lax.fori_loop(1, npass_ref[i], extra_pass, acc)
    out_ref[...] = _rms(x2) * gf_ref[...]


def _combine(base, npass, x1, rel3, gf, y, cap):
    nc, e, _ = rel3.shape
    n, d = x1.shape
    m = SLOT_WINDOW
    grid_spec = pltpu.PrefetchScalarGridSpec(
        num_scalar_prefetch=2,
        grid=(nc,),
        in_specs=[
            pl.BlockSpec((ROUTE_TILE, d), lambda i, *_: (i, 0)),
            pl.BlockSpec((1, e, ROUTE_TILE), lambda i, *_: (i, 0, 0)),
            pl.BlockSpec((1, d), lambda i, *_: (0, 0)),
            pl.BlockSpec(memory_space=pl.ANY),
        ],
        out_specs=pl.BlockSpec((ROUTE_TILE, d), lambda i, *_: (i, 0)),
        scratch_shapes=[
            pltpu.VMEM((COMBINE_AHEAD + 1, e * m, d), BF16),
            pltpu.SemaphoreType.DMA((COMBINE_AHEAD + 1,)),
        ],
    )
    return pl.pallas_call(
        functools.partial(_combine_kernel, e=e, cap=cap, nc=nc),
        grid_spec=grid_spec,
        out_shape=jax.ShapeDtypeStruct((n, d), F32),
        compiler_params=pltpu.CompilerParams(
            dimension_semantics=("arbitrary",), vmem_limit_bytes=V7X_VMEM_LIMIT),
        name="combine",
    )(base, npass, x1, rel3, gf, y)


def _largest_tile(total, limit, multiple):
    t = min(total, limit)
    while total % t or t % multiple:
        t -= 1
    return t


def _trunk(x, w, cap_factor, experts):
    b, s, d = x.shape
    n = b * s
    e = w["wrt"].shape[0]
    dc = w["cw"].shape[1]
    ds = w["sw"].shape[1]
    cap = cap_factor * n // e
    assert n % ROUTE_TILE == 0 and s % ROUTE_TILE == 0 and cap >= SLOT_WINDOW
    assert cap % SLOT_ALIGN == 0

    tm = _largest_tile(n, 256, 8)
    ts = _largest_tile(s, 256, ROUTE_TILE)
    flat = lambda v: v.reshape(-1, v.shape[-1])
    ride_in = [] if experts else [flat(w["wg32"]), flat(w["wu32"])]
    ride_mix = [] if experts else [flat(w["wd32"])]
    u, cast_in = _inproj(x.reshape(n, d), w["g1"], w["w_in"], dc, ds, tm, ride_in)
    (x1, h2, aff3), cast_mix = _mixer(
        u.reshape(b, s, dc + 2 * ds), x, w["cw"], w["cb"], w["lg"], w["lb"], w["sw"], w["sb"],
        w["goc"], w["gos"], w["w_out"], w["g2"], w["wrt"], ts, ride_mix)
    if not experts:
        experts = tuple(v.reshape(w["wg32"].shape[0], -1, v.shape[-1])
                        for v in (*cast_in, *cast_mix))
    wg, wu, wd = experts
    rel3, gate3, base3 = _route(aff3, cap)

    base = jnp.concatenate([base3[:, :, 0], jnp.full((1, e), cap, I32)], axis=0)
    rows = base[1:] - _align_down(base[:-1])
    npass = jnp.maximum((jnp.max(rows, axis=1) + SLOT_WINDOW - 1) // SLOT_WINDOW, 1).astype(I32)
    base = base.reshape(-1)

    xe, gs = _dispatch(base, npass, h2, rel3, gate3, cap)
    rt = _largest_tile(cap, 1024, SLOT_ALIGN)
    fc = _largest_tile(wg.shape[2], 512, 128)
    oc = _largest_tile(d, 1024, 128)
    y = _ffn(xe, gs, wg, wu, wd, cap, rt, fc, oc)
    out = _combine(base, npass, x1.reshape(n, d), rel3, w["gf"], y, cap)
    return out.reshape(b, s, d), experts


def kernel(x_prompt, x_sample, g_norm1, w_in, conf_dw_w, conf_dw_b, conf_ln_g, conf_ln_b,
           sc_dw_w, sc_dw_b, g_out_conf, g_out_sc, w_out, g_norm2, w_router, w_gate, w_up,
           w_down, g_final):
    depth = w_in.shape[0]
    cap_factor = 2
    row = lambda v: v.reshape(1, -1)
    layers = []
    for l in range(depth):
        layers.append(dict(
            g1=row(g_norm1[l]), w_in=w_in[l].astype(BF16),
            cw=conf_dw_w[l], cb=row(conf_dw_b[l]), lg=row(conf_ln_g[l]), lb=row(conf_ln_b[l]),
            sw=sc_dw_w[l], sb=row(sc_dw_b[l]), goc=row(g_out_conf[l]), gos=row(g_out_sc[l]),
            w_out=w_out[l].astype(BF16), g2=row(g_norm2[l]),
            wrt=w_router[l].T.astype(BF16),
            wg32=w_gate[l], wu32=w_up[l], wd32=w_down[l],
        ))
    assert depth == 1, "the final norm is fused into the last layer's combine"
    outs = []
    experts = None
    for x in (x_prompt, x_sample):
        w = dict(layers[0], gf=row(g_final))
        out, experts = _trunk(x, w, cap_factor, experts)
        outs.append(out)
    return tuple(outs)
```

```python
import functools

import jax
import jax.numpy as jnp
from jax import lax
from jax.experimental import pallas as pl
from jax.experimental.pallas import tpu as pltpu

F32 = jnp.float32
BF16 = jnp.bfloat16
I32 = jnp.int32

EPS = 1e-6
CONF_HALO = 16
SC_HALO = 8
SUBLANES = 8
CONV_ROWS = 32
ROUTE_TILE = 256
SLOT_WINDOW = 64
SLOT_ALIGN = 16
COMBINE_AHEAD = 3
UNSELECTED = -(1 << 20)
V7X_VMEM_LIMIT = 56 * 1024 * 1024


def _sigmoid(x):
    return 1.0 / (1.0 + jnp.exp(-x))


def _rms(x):
    return x * lax.rsqrt(jnp.mean(x * x, axis=-1, keepdims=True) + EPS)


def _rider_specs(riders, steps, step_of):
    in_specs, out_specs, out_shapes = [], [], []
    for w in riders:
        rows, cols = w.shape
        assert rows % (steps * SLOT_ALIGN) == 0, (rows, steps)
        spec = pl.BlockSpec((rows // steps, cols), lambda *g: (step_of(*g), 0))
        in_specs.append(spec)
        out_specs.append(spec)
        out_shapes.append(jax.ShapeDtypeStruct(w.shape, BF16))
    return in_specs, out_specs, out_shapes


def _cast_riders(srcs, dsts):
    for src, dst in zip(srcs, dsts):
        dst[...] = src[...].astype(BF16)


def _inproj_kernel(*refs, dc, ds, nride):
    x_ref, g_ref, w_ref = refs[:3]
    u_ref = refs[3 + nride]
    _cast_riders(refs[3:3 + nride], refs[4 + nride:])
    h = (_rms(x_ref[...]) * g_ref[...]).astype(BF16)

    def proj(c0, width):
        return jnp.dot(h, w_ref[:, c0:c0 + width], preferred_element_type=F32)

    u_ref[:, 0:dc] = proj(0, dc) * _sigmoid(proj(dc, dc))
    cg = proj(2 * dc + ds, ds)
    u_ref[:, dc:dc + ds] = cg * proj(2 * dc + 2 * ds, ds)
    u_ref[:, dc + ds:dc + 2 * ds] = proj(2 * dc, ds)


def _inproj(x2, g1, w_in, dc, ds, tm, riders=()):
    n, d = x2.shape
    cols = w_in.shape[1]
    r_in, r_out, r_shapes = _rider_specs(riders, n // tm, lambda i: i)
    outs = pl.pallas_call(
        functools.partial(_inproj_kernel, dc=dc, ds=ds, nride=len(riders)),
        grid=(n // tm,),
        in_specs=[
            pl.BlockSpec((tm, d), lambda i: (i, 0)),
            pl.BlockSpec((1, d), lambda i: (0, 0)),
            pl.BlockSpec((d, cols), lambda i: (0, 0), pipeline_mode=pl.Buffered(1)),
        ] + r_in,
        out_specs=[pl.BlockSpec((tm, dc + 2 * ds), lambda i: (i, 0))] + r_out,
        out_shape=[jax.ShapeDtypeStruct((n, dc + 2 * ds), F32)] + r_shapes,
        compiler_params=pltpu.CompilerParams(
            dimension_semantics=("parallel",), vmem_limit_bytes=V7X_VMEM_LIMIT),
        name="inproj",
    )(x2, g1, w_in, *riders)
    return outs[0], outs[1:]


def _sc_shifts(ks):
    pad = (ks - 1) // 2
    return sorted({(SC_HALO + k - pad) % SUBLANES for k in range(ks)} - {0})


MIXER_INPUTS = 17


def _mixer_kernel(*refs, ts, dc, ds, nt, kc, ks, nride):
    (u_ref, ap_ref, an_ref, cp_ref, cn_ref, x_ref,
     cw_ref, cb_ref, lg_ref, lb_ref, sw_ref, sb_ref, goc_ref, gos_ref,
     wout_ref, g2_ref, wrt_ref) = refs[:MIXER_INPUTS]
    outs = refs[MIXER_INPUTS + nride:]
    x1_ref, h2_ref, aff_ref = outs[:3]
    aext_ref, cext_ref, ash_ref, csh_ref, ya_ref, ys_ref = outs[3 + nride:]
    _cast_riders(refs[MIXER_INPUTS:MIXER_INPUTS + nride], outs[3:3 + nride])
    t = pl.program_id(1)
    keep_prev = (t > 0).astype(F32)
    keep_next = (t < nt - 1).astype(F32)
    aext_ref[0:CONF_HALO, :] = ap_ref[0] * keep_prev
    aext_ref[CONF_HALO:CONF_HALO + ts, :] = u_ref[0, :, 0:dc]
    aext_ref[CONF_HALO + ts:2 * CONF_HALO + ts, :] = an_ref[0] * keep_next
    cext_ref[0:SC_HALO, :] = cp_ref[0] * keep_prev
    cext_ref[SC_HALO:SC_HALO + ts, :] = u_ref[0, :, dc:dc + ds]
    cext_ref[SC_HALO + ts:2 * SC_HALO + ts, :] = cn_ref[0] * keep_next

    pad_c = (kc - 1) // 2
    pad_s = (ks - 1) // 2
    for r in range(1, SUBLANES):
        ash_ref[r - 1] = aext_ref[r:r + ash_ref.shape[1], :]
    sc_shifts = _sc_shifts(ks)
    for slot, r in enumerate(sc_shifts):
        csh_ref[slot] = cext_ref[r:r + csh_ref.shape[1], :]

    groups = CONV_ROWS // SUBLANES

    def grouped(v):
        return v.reshape(groups, SUBLANES, v.shape[-1])

    def shifted(ext_ref, sh_ref, o, index):
        q, r = divmod(o, SUBLANES)
        rows = pl.ds(q * SUBLANES, CONV_ROWS)
        return grouped(ext_ref[rows, :] if r == 0 else sh_ref[index(r), rows, :])

    for c in range(ts // CONV_ROWS):
        r0 = c * CONV_ROWS
        conv = jnp.broadcast_to(sb_ref[...][None], (groups, SUBLANES, ds))
        for k in range(ks):
            tap = shifted(cext_ref, csh_ref, r0 + SC_HALO + k - pad_s, sc_shifts.index)
            conv = conv + sw_ref[k][None] * tap
        s = grouped(u_ref[0, r0:r0 + CONV_ROWS, dc + ds:dc + 2 * ds]) * conv
        ys_ref[r0:r0 + CONV_ROWS, :] = (
            _rms(s) * gos_ref[...][None]).reshape(CONV_ROWS, ds).astype(BF16)
    x1 = x_ref[0] + jnp.dot(ys_ref[...], wout_ref[dc:dc + ds, :], preferred_element_type=F32)

    for c in range(ts // CONV_ROWS):
        r0 = c * CONV_ROWS
        acc = jnp.broadcast_to(cb_ref[...][None], (groups, SUBLANES, dc))
        for k in range(kc):
            tap = shifted(aext_ref, ash_ref, r0 + CONF_HALO + k - pad_c, lambda r: r - 1)
            acc = acc + cw_ref[k][None] * tap
        mu = jnp.mean(acc, axis=-1, keepdims=True)
        cen = acc - mu
        var = jnp.mean(cen * cen, axis=-1, keepdims=True)
        a = cen * lax.rsqrt(var + EPS) * lg_ref[...][None] + lb_ref[...][None]
        a = a * _sigmoid(a)
        ya_ref[r0:r0 + CONV_ROWS, :] = (
            _rms(a) * goc_ref[...][None]).reshape(CONV_ROWS, dc).astype(BF16)
    x1 = x1 + jnp.dot(ya_ref[...], wout_ref[0:dc, :], preferred_element_type=F32)
    x1_ref[0] = x1
    h2 = (_rms(x1) * g2_ref[...]).astype(BF16)
    h2_ref[...] = h2
    logits = lax.dot_general(wrt_ref[...], h2, (((1,), (1,)), ((), ())),
                             preferred_element_type=F32)
    ex = jnp.exp(logits - jnp.max(logits, axis=0, keepdims=True))
    aff = ex / jnp.sum(ex, axis=0, keepdims=True)
    for j in range(ts // ROUTE_TILE):
        aff_ref[j] = aff[:, j * ROUTE_TILE:(j + 1) * ROUTE_TILE]


def _replicate_rows(v):
    return jnp.broadcast_to(v, (SUBLANES, v.shape[1]))


def _replicate_taps(w):
    return jnp.broadcast_to(w[:, None, :], (w.shape[0], SUBLANES, w.shape[1]))


def _mixer(u3, x3, cw, cb, lg, lb, sw, sb, goc, gos, wout, g2, wrt, ts, riders=()):
    b, s, d = x3.shape
    dc = cw.shape[1]
    ds = sw.shape[1]
    kc = cw.shape[0]
    ks = sw.shape[0]
    e = wrt.shape[0]
    nt = s // ts
    n = b * s
    hb_c = ts // CONF_HALO
    hb_s = ts // SC_HALO
    col_cv = dc // ds

    def const(shape):
        return pl.BlockSpec(shape, lambda i, t: (0,) * len(shape))

    in_specs = [
        pl.BlockSpec((1, ts, dc + 2 * ds), lambda i, t: (i, t, 0)),
        pl.BlockSpec((1, CONF_HALO, dc), lambda i, t: (i, jnp.maximum(t * hb_c - 1, 0), 0)),
        pl.BlockSpec((1, CONF_HALO, dc),
                     lambda i, t: (i, jnp.minimum((t + 1) * hb_c, s // CONF_HALO - 1), 0)),
        pl.BlockSpec((1, SC_HALO, ds),
                     lambda i, t: (i, jnp.maximum(t * hb_s - 1, 0), col_cv)),
        pl.BlockSpec((1, SC_HALO, ds),
                     lambda i, t: (i, jnp.minimum((t + 1) * hb_s, s // SC_HALO - 1), col_cv)),
        pl.BlockSpec((1, ts, d), lambda i, t: (i, t, 0)),
        const((kc, SUBLANES, dc)), const((SUBLANES, dc)), const((SUBLANES, dc)),
        const((SUBLANES, dc)),
        const((ks, SUBLANES, ds)), const((SUBLANES, ds)), const((SUBLANES, dc)),
        const((SUBLANES, ds)),
        pl.BlockSpec((dc + ds, d), lambda i, t: (0, 0), pipeline_mode=pl.Buffered(1)),
        const((1, d)), const((e, d)),
    ]
    out_specs = [
        pl.BlockSpec((1, ts, d), lambda i, t: (i, t, 0)),
        pl.BlockSpec((ts, d), lambda i, t: (i * nt + t, 0)),
        pl.BlockSpec((ts // ROUTE_TILE, e, ROUTE_TILE), lambda i, t: (i * nt + t, 0, 0)),
    ]
    out_shape = [
        jax.ShapeDtypeStruct((b, s, d), F32),
        jax.ShapeDtypeStruct((n, d), BF16),
        jax.ShapeDtypeStruct((n // ROUTE_TILE, e, ROUTE_TILE), F32),
    ]
    assert len(in_specs) == MIXER_INPUTS
    r_in, r_out, r_shapes = _rider_specs(riders, b * nt, lambda i, t: i * nt + t)
    outs = pl.pallas_call(
        functools.partial(_mixer_kernel, ts=ts, dc=dc, ds=ds, nt=nt, kc=kc, ks=ks,
                          nride=len(riders)),
        grid=(b, nt),
        in_specs=in_specs + r_in,
        out_specs=out_specs + r_out,
        out_shape=out_shape + r_shapes,
        scratch_shapes=[
            pltpu.VMEM((ts + 2 * CONF_HALO, dc), F32),
            pltpu.VMEM((ts + 2 * SC_HALO, ds), F32),
            pltpu.VMEM((SUBLANES - 1, ts + 2 * CONF_HALO - SUBLANES, dc), F32),
            pltpu.VMEM((len(_sc_shifts(ks)), ts + 2 * SC_HALO - SUBLANES, ds), F32),
            pltpu.VMEM((ts, dc), BF16),
            pltpu.VMEM((ts, ds), BF16),
        ],
        compiler_params=pltpu.CompilerParams(
            dimension_semantics=("parallel", "parallel"), vmem_limit_bytes=V7X_VMEM_LIMIT),
        name="mixer",
    )(u3, u3, u3, u3, u3, x3, _replicate_taps(cw), _replicate_rows(cb), _replicate_rows(lg),
      _replicate_rows(lb), _replicate_taps(sw), _replicate_rows(sb), _replicate_rows(goc),
      _replicate_rows(gos), wout, g2, wrt, *riders)
    return outs[:3], outs[3:]


def _route_kernel(aff_ref, rel_ref, gate_ref, base_ref, before_ref, *, cap, nc, e):
    capf = jnp.float32(cap)

    def count(pred):
        bits = pltpu.bitcast(aff_ref[...], I32)
        per_lane = jnp.sum(pred(bits).astype(F32), axis=0)
        return jnp.sum(per_lane, axis=-1, keepdims=True)

    def search(i, cur):
        cand = cur | lax.shift_left(jnp.int32(1), 30 - i)
        return jnp.where(count(lambda bits: bits >= cand[None]) >= capf, cand, cur)

    thr = lax.fori_loop(0, 31, search, jnp.zeros((e, 1), I32))
    need_eq = capf - count(lambda bits: bits > thr[None])

    row = lax.broadcasted_iota(I32, (ROUTE_TILE, ROUTE_TILE), 0)
    col = lax.broadcasted_iota(I32, (ROUTE_TILE, ROUTE_TILE), 1)
    tri = (row <= col).astype(BF16)

    def rank(flags):
        within = jnp.dot(flags.reshape(nc * e, ROUTE_TILE).astype(BF16), tri,
                         preferred_element_type=F32).reshape(nc, e, ROUTE_TILE)
        before_ref[...] = jnp.broadcast_to(within[:, :, ROUTE_TILE - 1:], (nc, e, 128))

        def tile_prefix(c, running):
            total = before_ref[c]
            before_ref[c] = running
            return running + total

        lax.fori_loop(0, nc, tile_prefix, jnp.zeros((e, 128), F32))
        before = before_ref[...][:, :, 0:1]
        return within - flags + before, before

    aff = aff_ref[...]
    bits = pltpu.bitcast(aff, I32)
    eq = bits == thr[None]
    eq_rank, _ = rank(eq.astype(F32))
    sel = (bits > thr[None]) | (eq & (eq_rank < need_eq[None]))
    slot, sel_before = rank(sel.astype(F32))
    window = jnp.floor(sel_before * (1.0 / SLOT_ALIGN)) * SLOT_ALIGN
    rel_ref[...] = jnp.where(sel, (slot - window).astype(I32), UNSELECTED)
    gate_ref[...] = jnp.where(sel, aff, 0.0)
    base_ref[...] = before_ref[...].astype(I32)


def _route(aff3, cap):
    nc, e, _ = aff3.shape
    full = lambda shape: pl.BlockSpec(shape, lambda: (0,) * len(shape))
    return pl.pallas_call(
        functools.partial(_route_kernel, cap=cap, nc=nc, e=e),
        in_specs=[full((nc, e, ROUTE_TILE))],
        out_specs=[full((nc, e, ROUTE_TILE)), full((nc, e, ROUTE_TILE)), full((nc, e, 128))],
        out_shape=[
            jax.ShapeDtypeStruct((nc, e, ROUTE_TILE), I32),
            jax.ShapeDtypeStruct((nc, e, ROUTE_TILE), F32),
            jax.ShapeDtypeStruct((nc, e, 128), I32),
        ],
        scratch_shapes=[pltpu.VMEM((nc, e, 128), F32)],
        compiler_params=pltpu.CompilerParams(vmem_limit_bytes=V7X_VMEM_LIMIT),
        name="route",
    )(aff3)


def _align_down(slot):
    shift = SLOT_ALIGN.bit_length() - 1
    return (slot >> shift) << shift


def _window_start(base_ref, i, ex, e):
    return _align_down(base_ref[i * e + ex])


def _dispatch_kernel(base_ref, npass_ref, h2_ref, rel_ref, gate_ref,
                     xe_ref, gs_ref,
                     stage_ref, gstage_ref, carry_ref, gcarry_ref, count_ref, sem,
                     *, e, cap, nc):
    m = SLOT_WINDOW
    i = pl.program_id(0)

    @pl.when(i == 0)
    def _():
        carry_ref[...] = jnp.zeros_like(carry_ref)
        gcarry_ref[...] = jnp.zeros_like(gcarry_ref)
        count_ref[0] = 0

    def copies(buf, ex, dst_row):
        rows = pl.ds(ex * m, m)
        return (
            pltpu.make_async_copy(stage_ref.at[buf, rows], xe_ref.at[ex, pl.ds(dst_row, m)],
                                  sem.at[0]),
            pltpu.make_async_copy(gstage_ref.at[buf, rows], gs_ref.at[ex, pl.ds(dst_row, m)],
                                  sem.at[1]),
        )

    def wait_all(buf):
        for ex in range(e):
            for cp in copies(buf, ex, 0):
                cp.wait()

    h2 = h2_ref[...]
    jota = lax.broadcasted_iota(I32, (e, m, ROUTE_TILE), 1)

    def one_pass(q, _):
        done = count_ref[0]
        buf = done % 2
        relq = rel_ref[0] - q * m
        hit = relq[:, None, :] == jota
        p = hit.astype(F32)
        stage_ref[buf] = jnp.dot(p.reshape(e * m, ROUTE_TILE).astype(BF16), h2,
                                 preferred_element_type=F32).astype(BF16)
        gsel = jnp.sum(p * gate_ref[0][:, None, :], axis=-1, keepdims=True)
        gstage_ref[buf] = jnp.broadcast_to(gsel, (e, m, 128)).reshape(e * m, 128)

        for ex in range(e):
            head = pl.ds(ex * m, SLOT_ALIGN)
            wp = _window_start(base_ref, i, ex, e)
            hi = base_ref[(i + 1) * e + ex]

            @pl.when(q == 0)
            def _():
                stage_ref[buf, head, :] = stage_ref[buf, head, :] + carry_ref[ex]
                gstage_ref[buf, head, :] = gstage_ref[buf, head, :] + gcarry_ref[ex]
                carry_ref[ex] = jnp.zeros((SLOT_ALIGN, carry_ref.shape[-1]), BF16)
                gcarry_ref[ex] = jnp.zeros((SLOT_ALIGN, 128), F32)

            off = _align_down(hi) - wp - q * m

            @pl.when((off >= 0) & (off < m))
            def _():
                tail = pl.ds(pl.multiple_of(ex * m + off, SLOT_ALIGN), SLOT_ALIGN)
                carry_ref[ex] = stage_ref[buf, tail, :]
                gcarry_ref[ex] = gstage_ref[buf, tail, :]

        @pl.when(done > 0)
        def _():
            wait_all(1 - buf)

        for ex in range(e):
            wp = _window_start(base_ref, i, ex, e)
            hi = base_ref[(i + 1) * e + ex]
            needed = (hi - wp > q * m) | (q == 0)
            dst = jnp.where(needed, wp + q * m, cap)
            for cp in copies(buf, ex, pl.multiple_of(dst, SLOT_ALIGN)):
                cp.start()
        count_ref[0] = done + 1
        return 0

    lax.fori_loop(0, npass_ref[i], one_pass, 0)

    @pl.when(i == nc - 1)
    def _():
        wait_all((count_ref[0] - 1) % 2)
        stage_ref[0] = jnp.zeros(stage_ref.shape[1:], BF16)
        gstage_ref[0] = jnp.zeros(gstage_ref.shape[1:], F32)
        for ex in range(e):
            for cp in copies(0, ex, cap):
                cp.start()
        wait_all(0)


def _dispatch(base, npass, h2, rel3, gate3, cap):
    nc, e, _ = rel3.shape
    d = h2.shape[1]
    m = SLOT_WINDOW
    grid_spec = pltpu.PrefetchScalarGridSpec(
        num_scalar_prefetch=2,
        grid=(nc,),
        in_specs=[
            pl.BlockSpec((ROUTE_TILE, d), lambda i, *_: (i, 0)),
            pl.BlockSpec((1, e, ROUTE_TILE), lambda i, *_: (i, 0, 0)),
            pl.BlockSpec((1, e, ROUTE_TILE), lambda i, *_: (i, 0, 0)),
        ],
        out_specs=[pl.BlockSpec(memory_space=pl.ANY), pl.BlockSpec(memory_space=pl.ANY)],
        scratch_shapes=[
            pltpu.VMEM((2, e * m, d), BF16),
            pltpu.VMEM((2, e * m, 128), F32),
            pltpu.VMEM((e, SLOT_ALIGN, d), BF16),
            pltpu.VMEM((e, SLOT_ALIGN, 128), F32),
            pltpu.SMEM((1,), I32),
            pltpu.SemaphoreType.DMA((2,)),
        ],
    )
    return pl.pallas_call(
        functools.partial(_dispatch_kernel, e=e, cap=cap, nc=nc),
        grid_spec=grid_spec,
        out_shape=[
            jax.ShapeDtypeStruct((e, cap + m, d), BF16),
            jax.ShapeDtypeStruct((e, cap + m, 128), F32),
        ],
        compiler_params=pltpu.CompilerParams(
            dimension_semantics=("arbitrary",), vmem_limit_bytes=V7X_VMEM_LIMIT),
        name="dispatch",
    )(base, npass, h2, rel3, gate3)


def _ffn_kernel(xe_ref, gs_ref, wg_ref, wu_ref, wd_ref, y_ref, hid_ref, *, nf, fc):
    p = pl.program_id(2)

    @pl.when(p < nf)
    def _():
        xe = xe_ref[0]
        g = jnp.dot(xe, wg_ref[0], preferred_element_type=F32)
        u = jnp.dot(xe, wu_ref[0], preferred_element_type=F32)
        hid_ref[p] = (g * _sigmoid(g) * u).astype(BF16)

    @pl.when(p >= nf)
    def _():
        y = jnp.dot(hid_ref[0], wd_ref[0, 0:fc, :], preferred_element_type=F32)
        for f in range(1, nf):
            y = y + jnp.dot(hid_ref[f], wd_ref[0, f * fc:(f + 1) * fc, :],
                            preferred_element_type=F32)
        y_ref[0] = (y * gs_ref[0][:, 0:1]).astype(BF16)


def _ffn(xe, gs, wg, wu, wd, cap, rt, fc, oc):
    e, _, d = xe.shape
    dff = wg.shape[2]
    nf = dff // fc
    no = d // oc
    out_col = lambda p: jnp.maximum(p - nf, 0)
    return pl.pallas_call(
        functools.partial(_ffn_kernel, nf=nf, fc=fc),
        grid=(e, cap // rt, nf + no),
        in_specs=[
            pl.BlockSpec((1, rt, d), lambda ex, r, p: (ex, r, 0)),
            pl.BlockSpec((1, rt, 128), lambda ex, r, p: (ex, r, 0)),
            pl.BlockSpec((1, d, fc), lambda ex, r, p: (ex, 0, jnp.minimum(p, nf - 1))),
            pl.BlockSpec((1, d, fc), lambda ex, r, p: (ex, 0, jnp.minimum(p, nf - 1))),
            pl.BlockSpec((1, dff, oc), lambda ex, r, p: (ex, 0, out_col(p))),
        ],
        out_specs=pl.BlockSpec((1, rt, oc), lambda ex, r, p: (ex, r, out_col(p))),
        out_shape=jax.ShapeDtypeStruct((e, cap, d), BF16),
        scratch_shapes=[pltpu.VMEM((nf, rt, fc), BF16)],
        compiler_params=pltpu.CompilerParams(
            dimension_semantics=("parallel", "parallel", "arbitrary"),
            vmem_limit_bytes=V7X_VMEM_LIMIT),
        name="ffn",
    )(xe, gs, wg, wu, wd)


def _combine_kernel(base_ref, npass_ref, x1_ref, rel_ref, gf_ref, y_ref, out_ref,
                    ystage_ref, sem, *, e, cap, nc):
    m = SLOT_WINDOW
    i = pl.program_id(0)
    ring = COMBINE_AHEAD + 1
    buf = i % ring
    jota = lax.broadcasted_iota(I32, (e, m, ROUTE_TILE), 1)
    eota = lax.broadcasted_iota(I32, (e, 1, 1), 0)

    def windows(tile, q, into):
        shift = jnp.zeros((e, 1, 1), I32)
        cps = []
        for ex in range(e):
            wp = _window_start(base_ref, tile, ex, e) + q * m
            rp = jnp.minimum(wp, cap - m)
            shift = jnp.where(eota == ex, wp - rp, shift)
            cps.append(pltpu.make_async_copy(
                y_ref.at[ex, pl.ds(pl.multiple_of(rp, SLOT_ALIGN), m)],
                ystage_ref.at[into, pl.ds(ex * m, m)], sem.at[into]))
        return cps, shift

    def expand(q, shift, acc):
        relq = (rel_ref[0] - q * m)[:, None, :]
        hit = (relq >= 0) & (relq < m) & (relq + shift == jota)
        p = hit.astype(F32).reshape(e * m, ROUTE_TILE).astype(BF16)
        return acc + lax.dot_general(p, ystage_ref[buf], (((0,), (0,)), ((), ())),
                                     preferred_element_type=F32)

    @pl.when(i == 0)
    def _():
        for tile in range(min(COMBINE_AHEAD, nc)):
            for cp in windows(tile, 0, tile)[0]:
                cp.start()

    @pl.when(i + COMBINE_AHEAD < nc)
    def _():
        for cp in windows(i + COMBINE_AHEAD, 0, (i + COMBINE_AHEAD) % ring)[0]:
            cp.start()

    cps, shift = windows(i, 0, buf)
    for cp in cps:
        cp.wait()
    acc = expand(0, shift, x1_ref[...])

    def extra_pass(q, acc):
        cps, shift = windows(i, q, buf)
        for cp in cps:
            cp.start()
        for cp in cps:
            cp.wait()
        return expand(q, shift, acc)

    x2 = lax.fori_loop(1, npass_ref[i], extra_pass, acc)
    out_ref[...] = _rms(x2) * gf_ref[...]


def _combine(base, npass, x1, rel3, gf, y, cap):
    nc, e, _ = rel3.shape
    n, d = x1.shape
    m = SLOT_WINDOW
    grid_spec = pltpu.PrefetchScalarGridSpec(
        num_scalar_prefetch=2,
        grid=(nc,),
        in_specs=[
            pl.BlockSpec((ROUTE_TILE, d), lambda i, *_: (i, 0)),
            pl.BlockSpec((1, e, ROUTE_TILE), lambda i, *_: (i, 0, 0)),
            pl.BlockSpec((1, d), lambda i, *_: (0, 0)),
            pl.BlockSpec(memory_space=pl.ANY),
        ],
        out_specs=pl.BlockSpec((ROUTE_TILE, d), lambda i, *_: (i, 0)),
        scratch_shapes=[
            pltpu.VMEM((COMBINE_AHEAD + 1, e * m, d), BF16),
            pltpu.SemaphoreType.DMA((COMBINE_AHEAD + 1,)),
        ],
    )
    return pl.pallas_call(
        functools.partial(_combine_kernel, e=e, cap=cap, nc=nc),
        grid_spec=grid_spec,
        out_shape=jax.ShapeDtypeStruct((n, d), F32),
        compiler_params=pltpu.CompilerParams(
            dimension_semantics=("arbitrary",), vmem_limit_bytes=V7X_VMEM_LIMIT),
        name="combine",
    )(base, npass, x1, rel3, gf, y)


def _largest_tile(total, limit, multiple):
    t = min(total, limit)
    while total % t or t % multiple:
        t -= 1
    return t


def _trunk(x, w, cap_factor, experts):
    b, s, d = x.shape
    n = b * s
    e = w["wrt"].shape[0]
    dc = w["cw"].shape[1]
    ds = w["sw"].shape[1]
    cap = cap_factor * n // e
    assert n % ROUTE_TILE == 0 and s % ROUTE_TILE == 0 and cap >= SLOT_WINDOW
    assert cap % SLOT_ALIGN == 0

    tm = _largest_tile(n, 256, 8)
    ts = _largest_tile(s, 256, ROUTE_TILE)
    flat = lambda v: v.reshape(-1, v.shape[-1])
    ride_in = [] if experts else [flat(w["wg32"]), flat(w["wu32"])]
    ride_mix = [] if experts else [flat(w["wd32"])]
    u, cast_in = _inproj(x.reshape(n, d), w["g1"], w["w_in"], dc, ds, tm, ride_in)
    (x1, h2, aff3), cast_mix = _mixer(
        u.reshape(b, s, dc + 2 * ds), x, w["cw"], w["cb"], w["lg"], w["lb"], w["sw"], w["sb"],
        w["goc"], w["gos"], w["w_out"], w["g2"], w["wrt"], ts, ride_mix)
    if not experts:
        experts = tuple(v.reshape(w["wg32"].shape[0], -1, v.shape[-1])
                        for v in (*cast_in, *cast_mix))
    wg, wu, wd = experts
    rel3, gate3, base3 = _route(aff3, cap)

    base = jnp.concatenate([base3[:, :, 0], jnp.full((1, e), cap, I32)], axis=0)
    rows = base[1:] - _align_down(base[:-1])
    npass = jnp.maximum((jnp.max(rows, axis=1) + SLOT_WINDOW - 1) // SLOT_WINDOW, 1).astype(I32)
    base = base.reshape(-1)

    xe, gs = _dispatch(base, npass, h2, rel3, gate3, cap)
    rt = _largest_tile(cap, 1024, SLOT_ALIGN)
    fc = _largest_tile(wg.shape[2], 512, 128)
    oc = _largest_tile(d, 1024, 128)
    y = _ffn(xe, gs, wg, wu, wd, cap, rt, fc, oc)
    out = _combine(base, npass, x1.reshape(n, d), rel3, w["gf"], y, cap)
    return out.reshape(b, s, d), experts


def kernel(x_prompt, x_sample, g_norm1, w_in, conf_dw_w, conf_dw_b, conf_ln_g, conf_ln_b,
           sc_dw_w, sc_dw_b, g_out_conf, g_out_sc, w_out, g_norm2, w_router, w_gate, w_up,
           w_down, g_final):
    depth = w_in.shape[0]
    cap_factor = 2
    row = lambda v: v.reshape(1, -1)
    layers = []
    for l in range(depth):
        layers.append(dict(
            g1=row(g_norm1[l]), w_in=w_in[l].astype(BF16),
            cw=conf_dw_w[l], cb=row(conf_dw_b[l]), lg=row(conf_ln_g[l]), lb=row(conf_ln_b[l]),
            sw=sc_dw_w[l], sb=row(sc_dw_b[l]), goc=row(g_out_conf[l]), gos=row(g_out_sc[l]),
            w_out=w_out[l].astype(BF16), g2=row(g_norm2[l]),
            wrt=w_router[l].T.astype(BF16),
            wg32=w_gate[l], wu32=w_up[l], wd32=w_down[l],
        ))
    assert depth == 1, "the final norm is fused into the last layer's combine"
    outs = []
    experts = None
    for x in (x_prompt, x_sample):
        w = dict(layers[0], gf=row(g_final))
        out, experts = _trunk(x, w, cap_factor, experts)
        outs.append(out)
    return tuple(outs)
```
